```python
import math
import jax
import jax.numpy as jnp
from jax import lax
import numpy as np

D_MODEL = 4096
BATCH = 4
SEQ = 2048
DEPTH = 4
DEC_BATCH = 8
DEC_SEQ = 1
PAST_LEN = 8192
PAGE_SIZE = 128

HEAD_DIM = 128
NSA_HEADS = 16
NSA_KV_HEADS = 4
NSA_GROUP = NSA_HEADS // NSA_KV_HEADS
NSA_BLOCK = 64
NSA_TOPK = 16
NSA_WINDOW = 512
NSA_FORCE = 1.0e4
SEL_QBLOCK = 32
WIN_QBLOCK = 128
GDN_HEADS = 16
GDN_CONV = 4
GDN_CHUNK = 64
CONF_CH = D_MODEL
CONF_WIDTH = 31
FFN_HIDDEN = -(-(8 * D_MODEL) // (3 * 256)) * 256
PLE_DIM = 256
N_EVEN = (DEPTH + 1) // 2
N_ODD = DEPTH // 2
MIX_WIDTH = (NSA_HEADS + GDN_HEADS) * HEAD_DIM
EVEN_SPLITS = (NSA_HEADS * HEAD_DIM, 6 * NSA_KV_HEADS * HEAD_DIM, 3 * NSA_HEADS,
               3 * GDN_HEADS * HEAD_DIM, GDN_HEADS * HEAD_DIM, GDN_HEADS, GDN_HEADS)
EVEN_COLS = sum(EVEN_SPLITS)
EPS = 1e-6
NEG = -1.0e30

kernel_name = 'hybrid_nsa_gdn_conformer_step'


def rmsnorm(x, g=None):
    xf = x.astype(jnp.float32)
    y = xf * lax.rsqrt(jnp.mean(xf * xf, axis=-1, keepdims=True) + EPS)
    if g is not None:
        y = y * g.astype(jnp.float32)
    return y.astype(x.dtype)


def layernorm(x, g, b):
    xf = x.astype(jnp.float32)
    xc = xf - jnp.mean(xf, axis=-1, keepdims=True)
    y = xc * lax.rsqrt(jnp.mean(xc * xc, axis=-1, keepdims=True) + EPS)
    return (y * g.astype(jnp.float32) + b.astype(jnp.float32)).astype(x.dtype)


def l2norm(x):
    return x * lax.rsqrt(jnp.sum(x * x, axis=-1, keepdims=True) + EPS)


def masked_softmax(s, mask):
    s = jnp.where(mask, s, NEG)
    m = jnp.max(s, axis=-1, keepdims=True)
    e = jnp.where(mask, jnp.exp(s - m), 0.0)
    return e / jnp.maximum(jnp.sum(e, axis=-1, keepdims=True), 1e-30)


def alibi_slopes(n):
    return jnp.asarray([2.0 ** (-8.0 * (h + 1) / n) for h in range(n)], jnp.float32)


def depthwise_causal(x_hist, w):
    c = x_hist.shape[-1]
    return lax.conv_general_dilated(x_hist, w[:, None, :].astype(x_hist.dtype), window_strides=(1,),
                                    padding='VALID', dimension_numbers=('NWC', 'WIO', 'NWC'),
                                    feature_group_count=c)


def split_cols(a, sizes):
    idx, acc = [], 0
    for s in sizes[:-1]:
        acc += s
        idx.append(acc)
    return jnp.split(a, idx, axis=-1)


def nsa_compressed_selected(q, kv, q0, cmp_w, slopes):
    f32 = jnp.float32
    B, Tq, G, R, hd = q.shape
    L = kv.shape[1]
    nb = -(-L // NSA_BLOCK)
    kv = jnp.pad(kv, ((0, 0), (0, nb * NSA_BLOCK - L), (0, 0), (0, 0), (0, 0)))
    kvb = kv.reshape(B, nb, NSA_BLOCK, 4, G, hd)
    k_cmp = jnp.einsum('bnigd,ig->bngd', kvb[:, :, :, 0], cmp_w[..., 0].astype(kv.dtype))
    v_cmp = jnp.einsum('bnigd,ig->bngd', kvb[:, :, :, 1], cmp_w[..., 1].astype(kv.dtype))
    k_sel = kvb[:, :, :, 2].transpose(0, 3, 1, 2, 4)
    v_sel = kvb[:, :, :, 3].transpose(0, 3, 1, 2, 4)
    blk = jnp.arange(nb)
    blk_end = (blk + 1) * NSA_BLOCK - 1
    scale = hd ** -0.5
    n_keep = min(NSA_TOPK, nb)
    qc = math.gcd(Tq, SEL_QBLOCK)
    nc = Tq // qc
    bi = jnp.arange(B)[:, None, None, None]
    gi = jnp.arange(G)[None, None, :, None]
    sl = slopes[None, None, :, :, None]

    def chunk(args):
        qx, pos = args
        s = jnp.einsum('bqgrd,bngd->bqgrn', qx, k_cmp).astype(f32) * scale
        dist_c = (pos[:, None] - blk_end[None, :]).astype(f32)
        s = s - sl * dist_c[None, :, None, None, :]
        valid = (blk_end[None, :] <= pos[:, None])[None, :, None, None, :]
        p = masked_softmax(s, valid)
        o_c = jnp.einsum('bqgrn,bngd->bqgrd', p.astype(v_cmp.dtype), v_cmp)
        imp = jnp.sum(p, axis=3)
        cur = pos // NSA_BLOCK
        forced = (blk[None, :] == 0) | (blk[None, :] == cur[:, None]) | (blk[None, :] == cur[:, None] - 1)
        future = blk[None, :] * NSA_BLOCK > pos[:, None]
        imp = jnp.where(forced[None, :, None, :], NSA_FORCE, imp)
        imp = jnp.where(future[None, :, None, :], -1.0, imp)
        _, idx = lax.top_k(imp, n_keep)
        ks = k_sel[bi, gi, idx].reshape(B, qc, G, n_keep * NSA_BLOCK, hd)
        vs = v_sel[bi, gi, idx].reshape(B, qc, G, n_keep * NSA_BLOCK, hd)
        spos = (idx[..., None] * NSA_BLOCK + jnp.arange(NSA_BLOCK)).reshape(B, qc, G, n_keep * NSA_BLOCK)
        s2 = jnp.einsum('bqgrd,bqgkd->bqgrk', qx, ks).astype(f32) * scale
        dist_s = (pos[None, :, None, None] - spos).astype(f32)
        s2 = s2 - sl * dist_s[:, :, :, None, :]
        valid2 = (spos <= pos[None, :, None, None])[:, :, :, None, :]
        p2 = masked_softmax(s2, valid2)
        o_s = jnp.einsum('bqgrk,bqgkd->bqgrd', p2.astype(vs.dtype), vs)
        return o_c, o_s

    q_ch = q.reshape(B, nc, qc, G, R, hd).transpose(1, 0, 2, 3, 4, 5)
    p_ch = (q0 + jnp.arange(Tq)).reshape(nc, qc)
    o_c, o_s = lax.map(chunk, (q_ch, p_ch))
    o_c = o_c.transpose(1, 0, 2, 3, 4, 5).reshape(B, Tq, G, R, hd)
    o_s = o_s.transpose(1, 0, 2, 3, 4, 5).reshape(B, Tq, G, R, hd)
    return o_c, o_s


def nsa_window(q, k, v, q0, k0, slopes):
    f32 = jnp.float32
    B, Tq, G, R, hd = q.shape
    qb = math.gcd(Tq, WIN_QBLOCK)
    nqb = Tq // qb
    span = qb + NSA_WINDOW - 1
    kp = jnp.pad(k, ((0, 0), (NSA_WINDOW, 0), (0, 0), (0, 0)))
    vp = jnp.pad(v, ((0, 0), (NSA_WINDOW, 0), (0, 0), (0, 0)))
    idx = (q0 - k0 + 1) + jnp.arange(nqb)[:, None] * qb + jnp.arange(span)[None, :]
    kb = kp[:, idx]
    vb = vp[:, idx]
    kpos = k0 - NSA_WINDOW + idx
    qpos = q0 + jnp.arange(Tq).reshape(nqb, qb)
    qx = q.reshape(B, nqb, qb, G, R, hd)
    s = jnp.einsum('bnqgrd,bnkgd->bnqgrk', qx, kb).astype(f32) * (hd ** -0.5)
    dist = qpos[:, :, None] - kpos[:, None, :]
    s = s - slopes[None, None, None, :, :, None] * dist.astype(f32)[None, :, :, None, None, :]
    valid = (kpos[:, None, :] >= k0) & (dist >= 0) & (dist < NSA_WINDOW)
    p = masked_softmax(s, valid[None, :, :, None, None, :])
    o = jnp.einsum('bnqgrk,bnkgd->bnqgrd', p.astype(vb.dtype), vb)
    return o.reshape(B, Tq, G, R, hd)


def gated_delta_chunked(q, k, v, g, beta, s0):
    f32 = jnp.float32
    B, T, H, dk = q.shape
    dv = v.shape[-1]
    C = math.gcd(T, GDN_CHUNK)
    N = T // C
    to_ch = lambda a: a.reshape(B, N, C, H, a.shape[-1]).transpose(0, 3, 1, 2, 4)
    qc, kc, vc = to_ch(q), to_ch(k), to_ch(v)
    gc = jnp.cumsum(g.reshape(B, N, C, H).transpose(0, 3, 1, 2), axis=-1)
    bc = beta.reshape(B, N, C, H).transpose(0, 3, 1, 2)
    tri = jnp.tril(jnp.ones((C, C), bool))
    strict = jnp.tril(jnp.ones((C, C), bool), -1)
    diff = gc[..., :, None] - gc[..., None, :]
    decay = jnp.where(tri, jnp.exp(jnp.where(tri, diff, 0.0)), 0.0)
    kb = kc * bc[..., None]
    vb = vc * bc[..., None]
    a_mat = jnp.where(strict, jnp.einsum('bhncd,bhnsd->bhncs', kb, kc) * decay, 0.0)
    eye = jnp.eye(C, dtype=f32)
    t_mat = lax.linalg.triangular_solve(eye + a_mat, jnp.broadcast_to(eye, a_mat.shape), left_side=True, lower=True)
    u = jnp.einsum('bhncs,bhnse->bhnce', t_mat, vb)
    w = jnp.einsum('bhncs,bhnsd->bhncd', t_mat, kb * jnp.exp(gc)[..., None])
    qk = jnp.where(tri, jnp.einsum('bhncd,bhnsd->bhncs', qc, kc) * decay, 0.0)
    xs = tuple(jnp.moveaxis(a, 2, 0) for a in (qc, kc, gc, u, w, qk))

    def step(s, inp):
        qn, kn, gn, un, wn, qkn = inp
        v_new = un - jnp.einsum('bhcd,bhde->bhce', wn, s)
        o = jnp.einsum('bhcd,bhde->bhce', qn * jnp.exp(gn)[..., None], s) + jnp.einsum('bhcs,bhse->bhce', qkn, v_new)
        gl = gn[..., -1:]
        s = s * jnp.exp(gl)[..., None] + jnp.einsum('bhcd,bhce->bhde', kn * jnp.exp(gl - gn)[..., None], v_new)
        return s, o

    s_fin, o = lax.scan(step, s0, xs)
    o = o.transpose(1, 0, 3, 2, 4).reshape(B, T, H, dv)
    return o, s_fin


def even_mixer(xn, w_in, cmp_w, conv_w, a_log, dt_bias, gdn_g, w_out, slopes, past):
    f32 = jnp.float32
    B, T, _ = xn.shape
    G, R, hd, H = NSA_KV_HEADS, NSA_GROUP, HEAD_DIM, GDN_HEADS
    q_n, kv_n, g_n, qkv_g, z_g, a_g, b_g = split_cols(xn @ w_in, EVEN_SPLITS)
    q = q_n.reshape(B, T, G, R, hd)
    kv = kv_n.reshape(B, T, 6, G, hd)
    kv_rows, win_rows = kv[:, :, :4], kv[:, :, 4:]
    if past is None:
        q0, k0 = 0, 0
        kv_all, win_all = kv_rows, win_rows
        s0 = jnp.zeros((B, H, hd, hd), f32)
        conv_hist = jnp.zeros((B, GDN_CONV - 1, qkv_g.shape[-1]), xn.dtype)
    else:
        kv_past, win_buf, s0, conv_hist = past
        q0 = kv_past.shape[1]
        k0 = q0 - win_buf.shape[1]
        kv_all = jnp.concatenate([kv_past, kv_rows], axis=1)
        win_all = jnp.concatenate([win_buf, win_rows], axis=1)
    o_cmp, o_sel = nsa_compressed_selected(q, kv_all, q0, cmp_w, slopes)
    o_win = nsa_window(q, win_all[:, :, 0], win_all[:, :, 1], q0, k0, slopes)
    gates = jax.nn.sigmoid(g_n.reshape(B, T, G, R, 3))
    o_nsa = gates[..., 0:1] * o_cmp + gates[..., 1:2] * o_sel + gates[..., 2:3] * o_win
    n_win = win_all.shape[1]
    new_win = win_all[:, n_win - min(NSA_WINDOW, n_win):]
    hist_all = jnp.concatenate([conv_hist, qkv_g], axis=1)
    qkv_c = jax.nn.silu(depthwise_causal(hist_all, conv_w)).reshape(B, T, 3, H, hd).astype(f32)
    new_hist = hist_all[:, T:]
    qg = l2norm(qkv_c[:, :, 0]) * (hd ** -0.5)
    kg = l2norm(qkv_c[:, :, 1])
    vg = qkv_c[:, :, 2]
    beta = jax.nn.sigmoid(b_g.astype(f32))
    gdec = -jnp.exp(a_log.astype(f32)) * jax.nn.softplus(a_g.astype(f32) + dt_bias.astype(f32))
    o_g, s_new = gated_delta_chunked(qg, kg, vg, gdec, beta, s0.astype(f32))
    o_g = rmsnorm(o_g.astype(xn.dtype), gdn_g) * jax.nn.silu(z_g.reshape(B, T, H, hd))
    mixed = jnp.concatenate([o_nsa.reshape(B, T, G * R * hd), o_g.reshape(B, T, H * hd)], axis=-1)
    return mixed @ w_out, (kv_rows, new_win, s_new.astype(xn.dtype), new_hist)


def conformer_conv(xn, w_pw1, w_dw, b_dw, ln_g, ln_b, w_pw2, hist):
    B, T, _ = xn.shape
    a, b = jnp.split(xn @ w_pw1, 2, axis=-1)
    u = a * jax.nn.sigmoid(b)
    if hist is None:
        hist = jnp.zeros((B, CONF_WIDTH - 1, u.shape[-1]), u.dtype)
    ucat = jnp.concatenate([hist, u], axis=1)
    c = depthwise_causal(ucat, w_dw) + b_dw
    y = jax.nn.silu(layernorm(c, ln_g, ln_b)) @ w_pw2
    return y, ucat[:, T:]


def ffn_ple(h, p_i, g_pre, g_post, w_gate, w_up, w_down, w_pg, w_pp):
    hn = rmsnorm(h, g_pre)
    f = (jax.nn.silu(hn @ w_gate) * (hn @ w_up)) @ w_down
    h = h + rmsnorm(f, g_post)
    return h + jax.nn.sigmoid(rmsnorm(h) @ w_pg) * (p_i @ w_pp)


def setup_inputs(seed: int = 0) -> dict:
    key = jax.random.key(seed)
    ks = jax.random.split(key, 32)
    f32 = jnp.float32
    hd = HEAD_DIM
    n_pages = PAST_LEN // PAGE_SIZE
    n_used = DEC_BATCH * n_pages
    n_pool = n_used + (n_used + 3) // 4
    win_buf = min(NSA_WINDOW, PAST_LEN)
    nrm = lambda k, shape, s: jax.random.normal(k, shape, f32) * s
    page_table = jax.random.permutation(ks[0], n_pool)[:n_used].reshape(DEC_BATCH, n_pages).astype(jnp.int32)
    dt = jnp.exp(jax.random.uniform(ks[14], (N_EVEN, GDN_HEADS), f32, math.log(1e-3), math.log(1e-1)))
    return {
        'x_prompt': nrm(ks[1], (BATCH, SEQ, D_MODEL), 1.0),
        'x_sample': nrm(ks[2], (DEC_BATCH, DEC_SEQ, D_MODEL), 1.0),
        'cache_nsa_kv': nrm(ks[3], (N_EVEN, n_pool, PAGE_SIZE, 4, NSA_KV_HEADS, hd), 1.0),
        'cache_nsa_win': nrm(ks[4], (N_EVEN, DEC_BATCH, win_buf, 2, NSA_KV_HEADS, hd), 1.0),
        'state_gdn': nrm(ks[5], (N_EVEN, DEC_BATCH, GDN_HEADS, hd, hd), hd ** -0.5),
        'state_gdn_conv': nrm(ks[6], (N_EVEN, DEC_BATCH, GDN_CONV - 1, 3 * GDN_HEADS * hd), 1.0),
        'state_conf_conv': nrm(ks[7], (N_ODD, DEC_BATCH, CONF_WIDTH - 1, CONF_CH), 0.5),
        'page_table': page_table,
        'p_prompt': nrm(ks[8], (DEPTH, BATCH, SEQ, PLE_DIM), 1.0),
        'p_sample': nrm(ks[9], (DEPTH, DEC_BATCH, DEC_SEQ, PLE_DIM), 1.0),
        'norm_g': 1.0 + nrm(ks[10], (DEPTH, 4, D_MODEL), 0.05),
        'w_in_even': nrm(ks[11], (N_EVEN, D_MODEL, EVEN_COLS), D_MODEL ** -0.5),
        'cmp_pool_w': (1.0 + nrm(ks[12], (N_EVEN, NSA_BLOCK, NSA_KV_HEADS, 2), 0.1)) / NSA_BLOCK,
        'gdn_conv_w': nrm(ks[13], (N_EVEN, GDN_CONV, 3 * GDN_HEADS * hd), GDN_CONV ** -0.5),
        'gdn_a_log': jnp.log(jax.random.uniform(ks[15], (N_EVEN, GDN_HEADS), f32, 1.0, 16.0)),
        'gdn_dt_bias': dt + jnp.log(-jnp.expm1(-dt)),
        'gdn_norm_g': 1.0 + nrm(ks[16], (N_EVEN, hd), 0.05),
        'w_out_even': nrm(ks[17], (N_EVEN, MIX_WIDTH, D_MODEL), MIX_WIDTH ** -0.5),
        'w_pw1': nrm(ks[18], (N_ODD, D_MODEL, 2 * CONF_CH), D_MODEL ** -0.5),
        'w_dw': nrm(ks[19], (N_ODD, CONF_WIDTH, CONF_CH), CONF_WIDTH ** -0.5),
        'b_dw': nrm(ks[20], (N_ODD, CONF_CH), 0.02),
        'conf_ln_g': 1.0 + nrm(ks[21], (N_ODD, CONF_CH), 0.05),
        'conf_ln_b': nrm(ks[22], (N_ODD, CONF_CH), 0.02),
        'w_pw2': nrm(ks[23], (N_ODD, CONF_CH, D_MODEL), CONF_CH ** -0.5),
        'w_ffn_gate': nrm(ks[24], (DEPTH, D_MODEL, FFN_HIDDEN), D_MODEL ** -0.5),
        'w_ffn_up': nrm(ks[25], (DEPTH, D_MODEL, FFN_HIDDEN), D_MODEL ** -0.5),
        'w_ffn_down': nrm(ks[26], (DEPTH, FFN_HIDDEN, D_MODEL), FFN_HIDDEN ** -0.5),
        'w_ple_gate': nrm(ks[27], (DEPTH, D_MODEL, D_MODEL), D_MODEL ** -0.5),
        'w_ple_proj': nrm(ks[28], (DEPTH, PLE_DIM, D_MODEL), PLE_DIM ** -0.5),
    }


def reference(x_prompt, x_sample, cache_nsa_kv, cache_nsa_win, state_gdn, state_gdn_conv, state_conf_conv,
              page_table, p_prompt, p_sample, norm_g, w_in_even, cmp_pool_w, gdn_conv_w, gdn_a_log,
              gdn_dt_bias, gdn_norm_g, w_out_even, w_pw1, w_dw, b_dw, conf_ln_g, conf_ln_b, w_pw2,
              w_ffn_gate, w_ffn_up, w_ffn_down, w_ple_gate, w_ple_proj):
    slopes = alibi_slopes(NSA_HEADS).reshape(NSA_KV_HEADS, NSA_GROUP)
    n_dec, n_pages = page_table.shape
    page_rows = cache_nsa_kv.shape[2]
    hp, hs = x_prompt, x_sample
    kv_p, kv_s, win_p, win_s, gdn_p, gdn_s, gconv_p, gconv_s, cconv_p, cconv_s = ([] for _ in range(10))
    for i in range(DEPTH):
        g_pre, g_post, f_pre, f_post = norm_g[i, 0], norm_g[i, 1], norm_g[i, 2], norm_g[i, 3]
        if i % 2 == 0:
            e = i // 2
            wts = (w_in_even[e], cmp_pool_w[e], gdn_conv_w[e], gdn_a_log[e], gdn_dt_bias[e], gdn_norm_g[e],
                   w_out_even[e], slopes)
            kv_past = cache_nsa_kv[e][page_table].reshape(n_dec, n_pages * page_rows, 4, NSA_KV_HEADS, HEAD_DIM)
            mp, st_p = even_mixer(rmsnorm(hp, g_pre), *wts, None)
            ms, st_s = even_mixer(rmsnorm(hs, g_pre), *wts,
                                  (kv_past, cache_nsa_win[e], state_gdn[e], state_gdn_conv[e]))
            kv_p.append(st_p[0]); win_p.append(st_p[1]); gdn_p.append(st_p[2]); gconv_p.append(st_p[3])
            kv_s.append(st_s[0]); win_s.append(st_s[1]); gdn_s.append(st_s[2]); gconv_s.append(st_s[3])
        else:
            o = i // 2
            wts = (w_pw1[o], w_dw[o], b_dw[o], conf_ln_g[o], conf_ln_b[o], w_pw2[o])
            mp, c_p = conformer_conv(rmsnorm(hp, g_pre), *wts, None)
            ms, c_s = conformer_conv(rmsnorm(hs, g_pre), *wts, state_conf_conv[o])
            cconv_p.append(c_p)
            cconv_s.append(c_s)
        hp = hp + rmsnorm(mp, g_post)
        hs = hs + rmsnorm(ms, g_post)
        fw = (w_ffn_gate[i], w_ffn_up[i], w_ffn_down[i], w_ple_gate[i], w_ple_proj[i])
        hp = ffn_ple(hp, p_prompt[i], f_pre, f_post, *fw)
        hs = ffn_ple(hs, p_sample[i], f_pre, f_post, *fw)
    return (hp, hs, jnp.stack(kv_p), jnp.stack(kv_s), jnp.stack(win_p), jnp.stack(win_s),
            jnp.stack(gdn_p), jnp.stack(gdn_s), jnp.stack(gconv_p), jnp.stack(gconv_s),
            jnp.stack(cconv_p), jnp.stack(cconv_s))
```

```python
import functools
import math

import numpy as np
import jax
import jax.numpy as jnp
from jax import lax
from jax.experimental import pallas as pl
from jax.experimental.pallas import tpu as pltpu

F32 = jnp.float32
BF16 = jnp.bfloat16
I32 = jnp.int32
HIGHEST = lax.Precision.HIGHEST

HEAD_DIM = 128
NSA_HEADS = 16
NSA_KV_HEADS = 4
NSA_GROUP = NSA_HEADS // NSA_KV_HEADS
NSA_BLOCK = 64
NSA_TOPK = 16
NSA_WINDOW = 512
NSA_FORCE = 1.0e4
GDN_HEADS = 16
GDN_CONV = 4
GDN_CHUNK = 64
CONF_WIDTH = 31
EPS = 1e-6
NEG = -1.0e30

SAMPLE_PAD = 16
V7X_VMEM_LIMIT = 58 * 1024 * 1024
NSA_QTILE = 128
SEL_KCHUNK = 512

QW = NSA_HEADS * HEAD_DIM
GW = NSA_GROUP * HEAD_DIM
KVW = NSA_KV_HEADS * HEAD_DIM
GH = GDN_HEADS * HEAD_DIM


def _cparams(sem):
    return pltpu.CompilerParams(dimension_semantics=sem, vmem_limit_bytes=V7X_VMEM_LIMIT)


def _divisor_tile(n, target, mult):
    best = None
    for t in range(mult, min(n, target) + 1, mult):
        if n % t == 0:
            best = t
    assert best is not None, (n, target, mult)
    return best


def _sigmoid(x):
    return 1.0 / (1.0 + jnp.exp(-x))


def _iota(shape, dim):
    return lax.broadcasted_iota(I32, shape, dim)


def _rms(x):
    return x * lax.rsqrt(jnp.mean(x * x, axis=-1, keepdims=True) + EPS)


def _msoftmax(s, mask):
    s = jnp.where(mask, s, NEG)
    m = jnp.max(s, axis=-1, keepdims=True)
    e = jnp.where(mask, jnp.exp(s - m), 0.0)
    return e / jnp.maximum(jnp.sum(e, axis=-1, keepdims=True), 1e-30)


def _dot_nt(a, b, **kw):
    return lax.dot_general(a, b, (((1,), (1,)), ((), ())), preferred_element_type=F32, **kw)


def _norm_kernel(h_ref, g_ref, o_ref):
    o_ref[...] = (_rms(h_ref[...]) * g_ref[...]).astype(o_ref.dtype)


def _addnorm_kernel(h_ref, f_ref, g1_ref, g2_ref, h2_ref, hn_ref):
    h2 = h_ref[...] + _rms(f_ref[...]) * g1_ref[...]
    h2_ref[...] = h2
    hn_ref[...] = (_rms(h2) * g2_ref[...]).astype(hn_ref.dtype)


def _ln_silu_kernel(c_ref, g_ref, b_ref, o_ref):
    x = c_ref[...]
    xc = x - jnp.mean(x, axis=-1, keepdims=True)
    y = xc * lax.rsqrt(jnp.mean(xc * xc, axis=-1, keepdims=True) + EPS)
    y = y * g_ref[...] + b_ref[...]
    o_ref[...] = (y * _sigmoid(y)).astype(o_ref.dtype)


def _row_call(kernel, mats, vecs, out_dtypes, name):
    mp, d = mats[0].shape
    tr = _divisor_tile(mp, 160, 16)
    mat_spec = pl.BlockSpec((tr, d), lambda i: (i, 0))
    vec_spec = pl.BlockSpec((1, d), lambda i: (0, 0))
    outs = [jax.ShapeDtypeStruct((mp, d), dt) for dt in out_dtypes]
    res = pl.pallas_call(
        kernel, grid=(mp // tr,),
        in_specs=[mat_spec] * len(mats) + [vec_spec] * len(vecs),
        out_specs=[mat_spec] * len(outs), out_shape=outs,
        compiler_params=_cparams(("parallel",)), name=name,
    )(*mats, *[v.reshape(1, d).astype(F32) for v in vecs])
    return res


def _norm(h, g):
    return _row_call(_norm_kernel, [h], [g], [BF16], "rmsnorm")[0]


def _addnorm(h, f, g1, g2):
    return _row_call(_addnorm_kernel, [h, f], [g1, g2], [F32, BF16], "add_rmsnorm")


def _ln_silu(c, g, b):
    return _row_call(_ln_silu_kernel, [c], [g, b], [BF16], "layernorm_silu")[0]


def _ep_plain(accs, res):
    return accs[0]


def _ep_sum(accs, res):
    return accs[0] + accs[1]


def _ep_swiglu(accs, res):
    return accs[0] * _sigmoid(accs[0]) * accs[1]


def _ep_glu(accs, res):
    return accs[0] * _sigmoid(accs[1])


def _ep_ple(accs, res):
    return res + _sigmoid(accs[0]) * accs[1]


def _mm_kernel(*refs, x_of_w, epilogue, has_res):
    nx = max(x_of_w) + 1
    nw = len(x_of_w)
    x_refs, w_refs = refs[:nx], refs[nx:nx + nw]
    pos = nx + nw
    res_ref = refs[pos] if has_res else None
    pos += int(has_res)
    o_ref = refs[pos]
    wb_refs = refs[pos + 1:]

    @pl.when(pl.program_id(1) == 0)
    def _():
        for w_ref, wb_ref in zip(w_refs, wb_refs):
            wb_ref[...] = w_ref[...].astype(BF16)

    accs = [jnp.dot(x_refs[xi][...], wb[...], preferred_element_type=F32) for xi, wb in zip(x_of_w, wb_refs)]
    res = res_ref[...] if has_res else None
    o_ref[...] = epilogue(accs, res).astype(o_ref.dtype)


def _matmul(xs, ws, x_of_w, epilogue, n_out, out_dtype, tn, name, res=None, tm_target=512):
    mp = xs[0].shape[0]
    tm = _divisor_tile(mp, tm_target, 16)
    assert n_out % tn == 0
    in_specs = [pl.BlockSpec((tm, x.shape[1]), lambda j, i: (i, 0)) for x in xs]
    arrs = list(xs)
    scratch = []
    for (w, k, rb, cb) in ws:
        in_specs.append(pl.BlockSpec((k, tn), functools.partial(lambda j, i, rb, cb: (rb, cb + j), rb=rb, cb=cb)))
        arrs.append(w)
        scratch.append(pltpu.VMEM((k, tn), BF16))
    if res is not None:
        in_specs.append(pl.BlockSpec((tm, tn), lambda j, i: (i, j)))
        arrs.append(res)
    return pl.pallas_call(
        functools.partial(_mm_kernel, x_of_w=tuple(x_of_w), epilogue=epilogue, has_res=res is not None),
        grid=(n_out // tn, mp // tm),
        in_specs=in_specs,
        out_specs=pl.BlockSpec((tm, tn), lambda j, i: (i, j)),
        out_shape=jax.ShapeDtypeStruct((mp, n_out), out_dtype),
        scratch_shapes=scratch,
        compiler_params=_cparams(("arbitrary", "arbitrary")), name=name,
    )(*arrs)


def _col_tile(n, target):
    return _divisor_tile(n, target, 128)


def _dwconv_prompt_kernel(x_ref, w_ref, b_ref, o_ref, xs_ref, *, width, seq, pad, rows, act):
    ct = x_ref.shape[1]
    xs_ref[0:pad, :] = jnp.zeros((pad, ct), F32)
    xs_ref[pad:pad + seq, :] = x_ref[...]
    bias = b_ref[...]

    def body(i, carry):
        r0 = pl.multiple_of(i * rows, rows)
        acc = jnp.zeros((rows, ct), F32) + bias
        win = xs_ref[pl.ds(r0, rows + pad), :]
        off = pad - (width - 1)
        for j in range(width):
            acc = acc + w_ref[pl.ds(j, 1), :] * win[off + j:off + j + rows]
        if act:
            acc = acc * _sigmoid(acc)
        o_ref[pl.ds(r0, rows), :] = acc
        return carry

    lax.fori_loop(0, seq // rows, body, 0)


def _dwconv_prompt(x, w, bias, col0, ncols, batch, seq, act, name):
    mp = x.shape[0]
    width = w.shape[0]
    ct = _col_tile(ncols, 512)
    assert col0 % ct == 0
    pad = -(-(width - 1) // 8) * 8
    rows = 32
    return pl.pallas_call(
        functools.partial(_dwconv_prompt_kernel, width=width, seq=seq, pad=pad, rows=rows, act=act),
        grid=(batch, ncols // ct),
        in_specs=[pl.BlockSpec((seq, ct), lambda b, j: (b, col0 // ct + j)),
                  pl.BlockSpec((width, ct), lambda b, j: (0, j)),
                  pl.BlockSpec((1, ct), lambda b, j: (0, j))],
        out_specs=pl.BlockSpec((seq, ct), lambda b, j: (b, j)),
        out_shape=jax.ShapeDtypeStruct((mp, ncols), F32),
        scratch_shapes=[pltpu.VMEM((pad + seq, ct), F32)],
        compiler_params=_cparams(("parallel", "parallel")), name=name,
    )(x, w, bias.reshape(1, ncols))


def _dwconv_sample_kernel(x_ref, h_ref, w_ref, b_ref, prev_ref, o_ref, *, width, nb, act):
    del prev_ref
    ct = x_ref.shape[1]
    acc = jnp.sum(h_ref[...] * w_ref[0:width - 1, :][None], axis=1)
    acc = acc + x_ref[0:nb, :] * w_ref[width - 1:width, :] + b_ref[...]
    if act:
        acc = acc * _sigmoid(acc)
    o_ref[0:nb, :] = acc
    o_ref[nb:, :] = jnp.zeros((SAMPLE_PAD - nb, ct), F32)


def _dwconv_sample(x, hist, w, bias, col0, ncols, prev, row0, act, name):
    nb = hist.shape[0]
    width = w.shape[0]
    ct = _col_tile(ncols, 512)
    rb = row0 // SAMPLE_PAD
    return pl.pallas_call(
        functools.partial(_dwconv_sample_kernel, width=width, nb=nb, act=act),
        grid=(ncols // ct,),
        in_specs=[pl.BlockSpec((SAMPLE_PAD, ct), lambda j: (rb, col0 // ct + j)),
                  pl.BlockSpec((nb, width - 1, ct), lambda j: (0, 0, j)),
                  pl.BlockSpec((width, ct), lambda j: (0, j)),
                  pl.BlockSpec((1, ct), lambda j: (0, j)),
                  pl.BlockSpec(memory_space=pl.ANY)],
        out_specs=pl.BlockSpec((SAMPLE_PAD, ct), lambda j: (rb, j)),
        out_shape=jax.ShapeDtypeStruct(prev.shape, F32),
        input_output_aliases={4: 0},
        compiler_params=_cparams(("parallel",)), name=name,
    )(x, hist, w, bias.reshape(1, ncols), prev)


def _pool_prompt_kernel(x_ref, w_ref, o_ref, *, nblk):
    x = x_ref[...].reshape(nblk, NSA_BLOCK, x_ref.shape[1])
    o_ref[...] = jnp.sum(x * w_ref[...][None], axis=1)


def _pool_prompt(proj, w_exp, batch, seq):
    nblk = seq // NSA_BLOCK
    return pl.pallas_call(
        functools.partial(_pool_prompt_kernel, nblk=nblk),
        grid=(batch,),
        in_specs=[pl.BlockSpec((seq, 2 * KVW), lambda b: (b, QW // (2 * KVW))),
                  pl.BlockSpec((NSA_BLOCK, 2 * KVW), lambda b: (0, 0))],
        out_specs=pl.BlockSpec((None, nblk, 2 * KVW), lambda b: (b, 0, 0)),
        out_shape=jax.ShapeDtypeStruct((batch, nblk, 2 * KVW), F32),
        compiler_params=_cparams(("parallel",)), name="nsa_pool_prompt",
    )(proj, w_exp)


def _nsa_prompt_kernel(sl_ref, q_ref, kc_ref, vc_ref, ks_ref, vs_ref, kw_ref, vw_ref, gt_ref, o_ref, *, tq, seq):
    g = pl.program_id(1)
    p0 = pl.program_id(2) * tq
    nblk = seq // NSA_BLOCK
    n_keep = min(NSA_TOPK, nblk)
    scale = HEAD_DIM ** -0.5
    rr = NSA_GROUP

    q = q_ref[...]
    qs = jnp.concatenate([q[:, r * HEAD_DIM:(r + 1) * HEAD_DIM] for r in range(rr)], axis=0).astype(BF16)
    pos = p0 + _iota((tq, 1), 0)
    pos4 = jnp.concatenate([pos] * rr, axis=0)
    slope4 = jnp.concatenate([jnp.full((tq, 1), sl_ref[g, r], F32) for r in range(rr)], axis=0)

    kc = kc_ref[...].astype(BF16)
    vc = vc_ref[...].astype(BF16)
    blk = _iota((1, nblk), 1)
    blk_end = blk * NSA_BLOCK + (NSA_BLOCK - 1)
    s = _dot_nt(qs, kc) * scale - slope4 * (pos4 - blk_end).astype(F32)
    p = _msoftmax(s, blk_end <= pos4)
    o_c = jnp.dot(p.astype(BF16), vc, preferred_element_type=F32)

    imp = p[0:tq]
    for r in range(1, rr):
        imp = imp + p[r * tq:(r + 1) * tq]
    cur = pos // NSA_BLOCK
    forced = (blk == 0) | (blk == cur) | (blk == cur - 1)
    imp = jnp.where(forced, NSA_FORCE, imp)
    imp = jnp.where(blk * NSA_BLOCK > pos, -1.0, imp)
    rank = jnp.zeros((tq, nblk), I32)
    for i in range(nblk):
        c = imp[:, i:i + 1]
        rank = rank + ((c > imp) | ((c == imp) & (blk > i))).astype(I32)
    sel_f = (rank < n_keep).astype(BF16)

    kch = min(SEL_KCHUNK, seq)
    n_chunks = (p0 + tq + kch - 1) // kch

    def sel_body(c, carry):
        m, l, acc = carry
        k0 = pl.multiple_of(c * kch, kch)
        kk = ks_ref[pl.ds(k0, kch), :].astype(BF16)
        vv = vs_ref[pl.ds(k0, kch), :].astype(BF16)
        kpos = k0 + _iota((1, kch), 1)
        expand = (_iota((nblk, kch), 0) == ((k0 + _iota((nblk, kch), 1)) // NSA_BLOCK)).astype(BF16)
        selk = jnp.dot(sel_f, expand, preferred_element_type=F32) > 0.5
        mask1 = selk & (kpos <= pos)
        mask = jnp.concatenate([mask1] * rr, axis=0)
        sc = _dot_nt(qs, kk) * scale - slope4 * (pos4 - kpos).astype(F32)
        sc = jnp.where(mask, sc, NEG)
        m_new = jnp.maximum(m, jnp.max(sc, axis=-1, keepdims=True))
        alpha = jnp.exp(m - m_new)
        e = jnp.where(mask, jnp.exp(sc - m_new), 0.0)
        l = alpha * l + jnp.sum(e, axis=-1, keepdims=True)
        acc = alpha * acc + jnp.dot(e.astype(BF16), vv, preferred_element_type=F32)
        return m_new, l, acc

    init = (jnp.full((rr * tq, 1), NEG, F32), jnp.zeros((rr * tq, 1), F32), jnp.zeros((rr * tq, HEAD_DIM), F32))
    _, l, acc = lax.fori_loop(0, n_chunks, sel_body, init)
    o_s = acc / jnp.maximum(l, 1e-30)

    span = min(NSA_WINDOW + tq, seq)
    w0 = pl.multiple_of(jnp.clip(p0 + tq - span, 0, seq - span), 128)
    kk = kw_ref[pl.ds(w0, span), :].astype(BF16)
    vv = vw_ref[pl.ds(w0, span), :].astype(BF16)
    dist = pos4 - (w0 + _iota((1, span), 1))
    sw = _dot_nt(qs, kk) * scale - slope4 * dist.astype(F32)
    pw = _msoftmax(sw, (dist >= 0) & (dist < NSA_WINDOW))
    o_w = jnp.dot(pw.astype(BF16), vv, preferred_element_type=F32)

    gt = _sigmoid(gt_ref[...])
    for r in range(rr):
        rows = slice(r * tq, (r + 1) * tq)
        o_ref[:, r * HEAD_DIM:(r + 1) * HEAD_DIM] = (gt[:, 3 * r:3 * r + 1] * o_c[rows]
                                                    + gt[:, 3 * r + 1:3 * r + 2] * o_s[rows]
                                                    + gt[:, 3 * r + 2:3 * r + 3] * o_w[rows])


def _nsa_prompt(proj, kcvc, gates_t, slopes, batch, seq):
    mp = proj.shape[0]
    tq = min(NSA_QTILE, seq)
    nq = seq // tq
    nblk = seq // NSA_BLOCK
    hb = HEAD_DIM
    c_sel = (QW + 2 * KVW) // hb
    c_win = (QW + 4 * KVW) // hb
    g4 = NSA_KV_HEADS
    return pl.pallas_call(
        functools.partial(_nsa_prompt_kernel, tq=tq, seq=seq),
        grid=(batch, NSA_KV_HEADS, nq),
        in_specs=[pl.BlockSpec(memory_space=pltpu.SMEM),
                  pl.BlockSpec((tq, GW), lambda b, g, i: (b * nq + i, g)),
                  pl.BlockSpec((None, nblk, hb), lambda b, g, i: (b, 0, g)),
                  pl.BlockSpec((None, nblk, hb), lambda b, g, i: (b, 0, g4 + g)),
                  pl.BlockSpec((seq, hb), lambda b, g, i: (b, c_sel + g)),
                  pl.BlockSpec((seq, hb), lambda b, g, i: (b, c_sel + g4 + g)),
                  pl.BlockSpec((seq, hb), lambda b, g, i: (b, c_win + g)),
                  pl.BlockSpec((seq, hb), lambda b, g, i: (b, c_win + g4 + g)),
                  pl.BlockSpec((None, tq, 3 * NSA_GROUP), lambda b, g, i: (g, b * nq + i, 0))],
        out_specs=pl.BlockSpec((tq, GW), lambda b, g, i: (b * nq + i, g)),
        out_shape=jax.ShapeDtypeStruct((mp, QW), F32),
        compiler_params=_cparams(("parallel", "parallel", "arbitrary")), name="nsa_prompt",
    )(slopes, proj, kcvc, kcvc, proj, proj, proj, proj, gates_t)


def _pool_sample_kernel(pt_ref, c_ref, wk_ref, wv_ref, ko_ref, vo_ref):
    del pt_ref
    for half in range(c_ref.shape[0] // NSA_BLOCK):
        rows = pl.ds(half * NSA_BLOCK, NSA_BLOCK)
        ko_ref[half] = jnp.sum(c_ref[rows, 0] * wk_ref[...], axis=0)
        vo_ref[half] = jnp.sum(c_ref[rows, 1] * wv_ref[...], axis=0)


def _pool_sample(cache, e, page_table, wk, wv):
    nd, n_pages = page_table.shape
    page_rows = cache.shape[2]
    per = page_rows // NSA_BLOCK
    kvh, hd = cache.shape[4], cache.shape[5]
    out = jax.ShapeDtypeStruct((nd, n_pages * per, kvh, hd), F32)
    grid_spec = pltpu.PrefetchScalarGridSpec(
        num_scalar_prefetch=1, grid=(nd, n_pages),
        in_specs=[pl.BlockSpec((None, None, page_rows, 2, kvh, hd), lambda b, p, pt: (e, pt[b, p], 0, 0, 0, 0)),
                  pl.BlockSpec((NSA_BLOCK, kvh, hd), lambda b, p, pt: (0, 0, 0)),
                  pl.BlockSpec((NSA_BLOCK, kvh, hd), lambda b, p, pt: (0, 0, 0))],
        out_specs=[pl.BlockSpec((None, per, kvh, hd), lambda b, p, pt: (b, p, 0, 0))] * 2)
    return pl.pallas_call(_pool_sample_kernel, grid_spec=grid_spec, out_shape=[out, out],
                          compiler_params=_cparams(("parallel", "parallel")), name="nsa_pool_sample",
                          )(page_table, cache, wk, wv)


def _head_rows(row, n):
    return jnp.concatenate([row[:, h * HEAD_DIM:(h + 1) * HEAD_DIM] for h in range(n)], axis=0)


def _nsa_cmp_sample_kernel(sl_ref, q_ref, kc_ref, vc_ref, oc_ref, oh_ref, *, past, n_sel):
    b = pl.program_id(0)
    nblk = kc_ref.shape[0]
    scale = HEAD_DIM ** -0.5
    rr = NSA_GROUP
    qh = _head_rows(q_ref[pl.ds(b, 1), :], NSA_HEADS).astype(BF16)
    blk = _iota((1, nblk), 1)
    dist = (past - (blk * NSA_BLOCK + (NSA_BLOCK - 1))).astype(F32)
    last = (past // NSA_BLOCK) - 1
    for g in range(NSA_KV_HEADS):
        qg = jnp.concatenate([qh[g * rr:(g + 1) * rr], jnp.zeros((8 - rr, HEAD_DIM), BF16)], axis=0)
        slope = jnp.concatenate([jnp.full((1, 1), sl_ref[g, r], F32) for r in range(rr)]
                                + [jnp.zeros((8 - rr, 1), F32)], axis=0)
        kc = kc_ref[:, g, :].astype(BF16)
        vc = vc_ref[:, g, :].astype(BF16)
        s = _dot_nt(qg, kc) * scale - slope * dist
        p = _msoftmax(s, blk >= 0)
        oc_ref[g * 8:(g + 1) * 8, :] = jnp.dot(p.astype(BF16), vc, preferred_element_type=F32)
        imp = jnp.sum(p[0:rr], axis=0, keepdims=True)
        imp = jnp.where((blk == 0) | (blk == last), NSA_FORCE, imp)
        impb = jnp.broadcast_to(imp, (nblk, nblk))
        eye = _iota((nblk, nblk), 0) == _iota((nblk, nblk), 1)
        impc = jnp.sum(jnp.where(eye, impb, 0.0), axis=1, keepdims=True)
        before = (impc > impb) | ((impc == impb) & (_iota((nblk, nblk), 0) < _iota((nblk, nblk), 1)))
        rank = jnp.sum(before.astype(F32), axis=0, keepdims=True)
        sel = rank < n_sel
        lower = (_iota((nblk, nblk), 0) < _iota((nblk, nblk), 1)).astype(BF16)
        slot = jnp.dot(jnp.broadcast_to(sel.astype(BF16), (8, nblk)), lower, preferred_element_type=F32)[0:1]
        onehot = sel & (jnp.abs(slot - _iota((NSA_TOPK, nblk), 0).astype(F32)) < 0.5)
        idx = jnp.sum(jnp.where(onehot, _iota((NSA_TOPK, nblk), 1), 0), axis=1, keepdims=True)
        oh_ref[g] = jnp.broadcast_to(idx, (NSA_TOPK, HEAD_DIM))


def _nsa_cmp_sample(proj, kcs, vcs, slopes, row0, past):
    nd, nblk = kcs.shape[0], kcs.shape[1]
    n_sel = min(NSA_TOPK, nblk + 1) - 1
    assert past % NSA_BLOCK == 0 and nblk >= NSA_TOPK
    rb = row0 // SAMPLE_PAD
    return pl.pallas_call(
        functools.partial(_nsa_cmp_sample_kernel, past=past, n_sel=n_sel),
        grid=(nd,),
        in_specs=[pl.BlockSpec(memory_space=pltpu.SMEM),
                  pl.BlockSpec((SAMPLE_PAD, QW), lambda b: (rb, 0)),
                  pl.BlockSpec((None, nblk, NSA_KV_HEADS, HEAD_DIM), lambda b: (b, 0, 0, 0)),
                  pl.BlockSpec((None, nblk, NSA_KV_HEADS, HEAD_DIM), lambda b: (b, 0, 0, 0))],
        out_specs=[pl.BlockSpec((None, 8 * NSA_KV_HEADS, HEAD_DIM), lambda b: (b, 0, 0)),
                   pl.BlockSpec((None, NSA_KV_HEADS, NSA_TOPK, HEAD_DIM), lambda b: (b, 0, 0, 0))],
        out_shape=[jax.ShapeDtypeStruct((nd, 8 * NSA_KV_HEADS, HEAD_DIM), F32),
                   jax.ShapeDtypeStruct((nd, NSA_KV_HEADS, NSA_TOPK, HEAD_DIM), I32)],
        compiler_params=_cparams(("parallel",)), name="nsa_cmp_sample",
    )(slopes, proj, kcs, vcs)


def _nsa_sel_sample_kernel(idx_ref, pt_ref, sl_ref, proj_ref, c0_ref, c1_ref, c2_ref, c3_ref, win_ref, oc_ref,
                           gt_ref, prev_ref, o_ref, m_ref, l_ref, acc_ref, *, past, n_sel, nd):
    del pt_ref, prev_ref
    b = pl.program_id(0)
    j = pl.program_id(1)
    scale = HEAD_DIM ** -0.5
    rr = NSA_GROUP
    c_refs = (c0_ref, c1_ref, c2_ref, c3_ref)

    @pl.when((b == 0) & (j == 0))
    def _():
        o_ref[...] = jnp.zeros(o_ref.shape, F32)

    @pl.when(j == 0)
    def _():
        m_ref[...] = jnp.full(m_ref.shape, NEG, F32)
        l_ref[...] = jnp.zeros(l_ref.shape, F32)
        acc_ref[...] = jnp.zeros(acc_ref.shape, F32)

    row = proj_ref[pl.ds(b, 1), :]
    qh = _head_rows(row[:, 0:QW], NSA_HEADS).astype(BF16)

    def group_q(g):
        qg = jnp.concatenate([qh[g * rr:(g + 1) * rr], jnp.zeros((8 - rr, HEAD_DIM), BF16)], axis=0)
        slope = jnp.concatenate([jnp.full((1, 1), sl_ref[g, r], F32) for r in range(rr)]
                                + [jnp.zeros((8 - rr, 1), F32)], axis=0)
        return qg, slope

    for g in range(NSA_KV_HEADS):
        qg, slope = group_q(g)
        blk_id = idx_ref[(b * NSA_KV_HEADS + g) * NSA_TOPK + j]
        kk = c_refs[g][:, 0, g, :].astype(BF16)
        vv = c_refs[g][:, 1, g, :].astype(BF16)
        dist = (past - (blk_id * NSA_BLOCK + _iota((1, NSA_BLOCK), 1))).astype(F32)
        sc = _dot_nt(qg, kk) * scale - slope * dist
        rows = slice(g * 8, (g + 1) * 8)
        m_old = m_ref[rows, 0:1]
        m_new = jnp.maximum(m_old, jnp.max(sc, axis=-1, keepdims=True))
        alpha = jnp.exp(m_old - m_new)
        e = jnp.exp(sc - m_new)
        l_ref[rows, :] = alpha * l_ref[rows, :] + jnp.sum(e, axis=-1, keepdims=True)
        acc_ref[rows, :] = alpha * acc_ref[rows, :] + jnp.dot(e.astype(BF16), vv, preferred_element_type=F32)
        m_ref[rows, :] = jnp.broadcast_to(m_new, (8, HEAD_DIM))

    @pl.when(j == n_sel - 1)
    def _():
        gt = _sigmoid(gt_ref[pl.ds(b, 1), :])
        pieces = []
        for g in range(NSA_KV_HEADS):
            qg, slope = group_q(g)
            qf = qg.astype(F32)
            base = QW + g * HEAD_DIM
            k_sel = row[:, base + 2 * KVW: base + 2 * KVW + HEAD_DIM].astype(BF16).astype(F32)
            v_sel = row[:, base + 3 * KVW: base + 3 * KVW + HEAD_DIM].astype(BF16).astype(F32)
            k_win = row[:, base + 4 * KVW: base + 4 * KVW + HEAD_DIM].astype(BF16).astype(F32)
            v_win = row[:, base + 5 * KVW: base + 5 * KVW + HEAD_DIM].astype(BF16).astype(F32)
            rows = slice(g * 8, (g + 1) * 8)
            s_new = jnp.sum(qf * k_sel, axis=-1, keepdims=True) * scale
            m_old = m_ref[rows, 0:1]
            m_new = jnp.maximum(m_old, s_new)
            alpha = jnp.exp(m_old - m_new)
            e_new = jnp.exp(s_new - m_new)
            l = alpha * l_ref[rows, 0:1] + e_new
            o_s = (alpha * acc_ref[rows, :] + e_new.astype(BF16).astype(F32) * v_sel) / jnp.maximum(l, 1e-30)
            nwin = win_ref.shape[0]
            kw = win_ref[:, 0, g, :].astype(BF16)
            vw = win_ref[:, 1, g, :].astype(BF16)
            dist = nwin - _iota((1, nwin), 1)
            sw = _dot_nt(qg, kw) * scale - slope * dist.astype(F32)
            valid = dist < NSA_WINDOW
            sw = jnp.where(valid, sw, NEG)
            sw_new = jnp.sum(qf * k_win, axis=-1, keepdims=True) * scale
            mw = jnp.maximum(jnp.max(sw, axis=-1, keepdims=True), sw_new)
            ew = jnp.where(valid, jnp.exp(sw - mw), 0.0)
            ew_new = jnp.exp(sw_new - mw)
            den = jnp.maximum(jnp.sum(ew, axis=-1, keepdims=True) + ew_new, 1e-30)
            o_w = (jnp.dot((ew / den).astype(BF16), vw, preferred_element_type=F32)
                   + (ew_new / den).astype(BF16).astype(F32) * v_win)
            o_c = oc_ref[rows, :]
            for r in range(rr):
                c = g * 3 * rr + 3 * r
                pieces.append(gt[:, c:c + 1] * o_c[r:r + 1] + gt[:, c + 1:c + 2] * o_s[r:r + 1]
                              + gt[:, c + 2:c + 3] * o_w[r:r + 1])
        o_ref[pl.ds(b, 1), :] = jnp.concatenate(pieces, axis=1)


def _nsa_sel_sample(proj, small, cache, win_cache, e, page_table, idx, o_cmp, slopes, prev, row0, past):
    nd = page_table.shape[0]
    page_rows = cache.shape[2]
    per = page_rows // NSA_BLOCK
    kvh, hd = cache.shape[4], cache.shape[5]
    nwin = win_cache.shape[2]
    n_sel = NSA_TOPK - 1
    rb = row0 // SAMPLE_PAD

    def cache_spec(g):
        def imap(b, j, idx_r, pt_r):
            blk = idx_r[(b * NSA_KV_HEADS + g) * NSA_TOPK + j]
            return (e, pt_r[b, blk // per], blk % per, 1, 0, 0)
        return pl.BlockSpec((None, None, NSA_BLOCK, 2, kvh, hd), imap)

    grid_spec = pltpu.PrefetchScalarGridSpec(
        num_scalar_prefetch=2, grid=(nd, n_sel),
        in_specs=[pl.BlockSpec(memory_space=pltpu.SMEM),
                  pl.BlockSpec((SAMPLE_PAD, proj.shape[1]), lambda b, j, i_r, p_r: (rb, 0)),
                  cache_spec(0), cache_spec(1), cache_spec(2), cache_spec(3),
                  pl.BlockSpec((None, None, nwin, 2, kvh, hd), lambda b, j, i_r, p_r: (e, b, 0, 0, 0, 0)),
                  pl.BlockSpec((None, 8 * NSA_KV_HEADS, HEAD_DIM), lambda b, j, i_r, p_r: (b, 0, 0)),
                  pl.BlockSpec((SAMPLE_PAD, small.shape[1]), lambda b, j, i_r, p_r: (rb, 0)),
                  pl.BlockSpec(memory_space=pl.ANY)],
        out_specs=pl.BlockSpec((SAMPLE_PAD, QW), lambda b, j, i_r, p_r: (rb, 0)),
        scratch_shapes=[pltpu.VMEM((8 * NSA_KV_HEADS, HEAD_DIM), F32)] * 3)
    return pl.pallas_call(
        functools.partial(_nsa_sel_sample_kernel, past=past, n_sel=n_sel, nd=nd),
        grid_spec=grid_spec,
        out_shape=jax.ShapeDtypeStruct(prev.shape, F32),
        input_output_aliases={11: 0},
        compiler_params=_cparams(("arbitrary", "arbitrary")), name="nsa_sel_sample",
    )(idx, page_table, slopes, proj, cache, cache, cache, cache, win_cache, o_cmp, small, prev)


def _stack_heads(x, n):
    return jnp.stack([x[:, h * HEAD_DIM:(h + 1) * HEAD_DIM] for h in range(n)], axis=0)


def _l2n(x):
    return x * lax.rsqrt(jnp.sum(x * x, axis=-1, keepdims=True) + EPS)


def _softplus(x):
    return jnp.maximum(x, 0.0) + jnp.log(1.0 + jnp.exp(-jnp.abs(x)))


def _bmm(a, b, **kw):
    return jnp.einsum('hcs,hsd->hcd', a, b, preferred_element_type=F32, **kw)


def _bmm_nt(a, b, **kw):
    return jnp.einsum('hcd,hsd->hcs', a, b, preferred_element_type=F32, **kw)


def _bmm_tn(a, b, **kw):
    return jnp.einsum('hcd,hce->hde', a, b, preferred_element_type=F32, **kw)


def _gdn_intra_kernel(x_ref, sm_ref, al_ref, dt_ref, qe_ref, ke_ref, u_ref, w_ref, qk_ref, egl_ref, *, a_col, b_col):
    nh, cc = GDN_HEADS, x_ref.shape[0]
    x = x_ref[...]
    q = _l2n(_stack_heads(x[:, 0:GH], nh)) * (HEAD_DIM ** -0.5)
    k = _l2n(_stack_heads(x[:, GH:2 * GH], nh))
    v = _stack_heads(x[:, 2 * GH:3 * GH], nh)
    sm = sm_ref[...]
    gdec = -jnp.exp(al_ref[...]) * _softplus(sm + dt_ref[...])
    ri, ci = _iota((cc, cc), 0), _iota((cc, cc), 1)
    tri = ri >= ci
    strict = ri > ci
    gcum = jnp.dot(tri.astype(F32), gdec, preferred_element_type=F32, precision=HIGHEST)
    beta = _sigmoid(sm)
    gc = jnp.stack([gcum[:, a_col + h:a_col + h + 1] for h in range(nh)], axis=0)
    bc = jnp.stack([beta[:, b_col + h:b_col + h + 1] for h in range(nh)], axis=0)
    gr = jnp.sum(jnp.where((ri == ci)[None], jnp.broadcast_to(gc, (nh, cc, cc)), 0.0), axis=1, keepdims=True)
    decay = jnp.where(tri[None], jnp.exp(jnp.where(tri[None], gc - gr, 0.0)), 0.0)
    kb = k * bc
    vb = v * bc
    a = jnp.where(strict[None], _bmm_nt(kb, k) * decay, 0.0)
    neg = -a
    tm = jnp.where((ri == ci)[None], 1.0, 0.0) + neg
    pw = neg
    span = 2
    while span < cc:
        pw = _bmm(pw, pw, precision=HIGHEST)
        tm = tm + _bmm(tm, pw, precision=HIGHEST)
        span *= 2
    eg = jnp.exp(gc)
    gl = gc[:, cc - 1:cc, :]
    u = _bmm(tm, vb)
    w = _bmm(tm, kb * eg)
    qk = jnp.where(tri[None], _bmm_nt(q, k) * decay, 0.0)
    qe = q * eg
    ke = k * jnp.exp(gl - gc)
    egl = jnp.exp(gl)
    for h in range(nh):
        cols = slice(h * HEAD_DIM, (h + 1) * HEAD_DIM)
        qe_ref[:, cols] = qe[h]
        ke_ref[:, cols] = ke[h]
        u_ref[:, cols] = u[h]
        w_ref[:, cols] = w[h]
        qk_ref[:, cols] = jnp.concatenate([qk[h], jnp.zeros((cc, HEAD_DIM - cc), F32)], axis=1)
        egl_ref[:, cols] = jnp.broadcast_to(egl[h], (8, HEAD_DIM))


def _gdn_intra(qkv_c, small, a_log_row, dt_row, n_rows, a_col, b_col):
    cc = GDN_CHUNK
    assert cc <= HEAD_DIM
    nch = n_rows // cc
    big = jax.ShapeDtypeStruct((n_rows, GH), F32)
    blk = pl.BlockSpec((cc, GH), lambda i: (i, 0))
    return pl.pallas_call(
        functools.partial(_gdn_intra_kernel, a_col=a_col, b_col=b_col),
        grid=(nch,),
        in_specs=[pl.BlockSpec((cc, 3 * GH), lambda i: (i, 0)),
                  pl.BlockSpec((cc, small.shape[1]), lambda i: (i, 0)),
                  pl.BlockSpec((1, small.shape[1]), lambda i: (0, 0)),
                  pl.BlockSpec((1, small.shape[1]), lambda i: (0, 0))],
        out_specs=[blk] * 5 + [pl.BlockSpec((None, 8, GH), lambda i: (i, 0, 0))],
        out_shape=[big] * 5 + [jax.ShapeDtypeStruct((nch, 8, GH), F32)],
        compiler_params=_cparams(("parallel",)), name="gdn_intra_chunk",
    )(qkv_c, small, a_log_row, dt_row)


def _gdn_scan_kernel(qe_ref, ke_ref, u_ref, w_ref, qk_ref, egl_ref, z_ref, g_ref, o_ref, s_ref):
    nh, cc = GDN_HEADS, qe_ref.shape[0]

    @pl.when(pl.program_id(1) == 0)
    def _():
        s_ref[...] = jnp.zeros(s_ref.shape, F32)

    s = s_ref[...]
    qe = _stack_heads(qe_ref[...], nh)
    ke = _stack_heads(ke_ref[...], nh)
    u = _stack_heads(u_ref[...], nh)
    w = _stack_heads(w_ref[...], nh)
    qk = _stack_heads(qk_ref[...], nh)[:, :, 0:cc]
    egl = _stack_heads(egl_ref[...], nh)[:, 0:1, :]
    v_new = u - _bmm(w, s)
    o = _bmm(qe, s) + _bmm(qk, v_new)
    s_ref[...] = s * egl + _bmm_tn(ke, v_new)
    z = _stack_heads(z_ref[...], nh)
    o = _rms(o) * g_ref[...][None] * (z * _sigmoid(z))
    for h in range(nh):
        o_ref[:, h * HEAD_DIM:(h + 1) * HEAD_DIM] = o[h]


def _gdn_scan(intra, gz, gdn_g, mp, batch, seq):
    cc = GDN_CHUNK
    nch = seq // cc
    qe, ke, u, w, qk, egl = intra
    blk = pl.BlockSpec((cc, GH), lambda b, n: (b * nch + n, 0))
    return pl.pallas_call(
        _gdn_scan_kernel,
        grid=(batch, nch),
        in_specs=[blk] * 5 + [pl.BlockSpec((None, 8, GH), lambda b, n: (b * nch + n, 0, 0)),
                              pl.BlockSpec((cc, GH), lambda b, n: (b * nch + n, 3)),
                              pl.BlockSpec((1, HEAD_DIM), lambda b, n: (0, 0))],
        out_specs=[blk, pl.BlockSpec((None, GDN_HEADS, HEAD_DIM, HEAD_DIM), lambda b, n: (b, 0, 0, 0))],
        out_shape=[jax.ShapeDtypeStruct((mp, GH), F32),
                   jax.ShapeDtypeStruct((batch, GDN_HEADS, HEAD_DIM, HEAD_DIM), F32)],
        compiler_params=_cparams(("parallel", "arbitrary")), name="gdn_scan",
    )(qe, ke, u, w, qk, egl, gz, gdn_g.reshape(1, HEAD_DIM))


def _col_of_row(row, n):
    eye = _iota((n, n), 0) == _iota((n, n), 1)
    return jnp.sum(jnp.where(eye, jnp.broadcast_to(row, (n, n)), 0.0), axis=1, keepdims=True)


def _gdn_sample_kernel(x_ref, h_ref, cw_ref, gz_ref, sm_ref, al_ref, dt_ref, g_ref, s0_ref, prev_ref,
                       o_ref, s_ref, *, a_col, b_col):
    del prev_ref
    b = pl.program_id(0)
    nh = GDN_HEADS

    @pl.when(b == 0)
    def _():
        o_ref[...] = jnp.zeros(o_ref.shape, F32)

    nk = cw_ref.shape[0]
    conv = jnp.sum(h_ref[...] * cw_ref[0:nk - 1, :], axis=0, keepdims=True) + x_ref[pl.ds(b, 1), :] * cw_ref[nk - 1:nk, :]
    conv = conv * _sigmoid(conv)
    q = _l2n(_head_rows(conv[:, 0:GH], nh)) * (HEAD_DIM ** -0.5)
    k = _l2n(_head_rows(conv[:, GH:2 * GH], nh))
    v = _head_rows(conv[:, 2 * GH:3 * GH], nh)
    sm = sm_ref[pl.ds(b, 1), :]
    gdec = -jnp.exp(al_ref[...]) * _softplus(sm + dt_ref[...])
    eg = jnp.exp(_col_of_row(gdec[:, a_col:a_col + nh], nh))
    beta = _col_of_row(_sigmoid(sm)[:, b_col:b_col + nh], nh)
    s = s0_ref[...]
    pad = lambda t: jnp.concatenate([t[:, None, :], jnp.zeros((nh, 7, HEAD_DIM), F32)], axis=1)
    ks = _bmm(pad(k), s, precision=HIGHEST)[:, 0, :]
    v_new = beta * (v - eg * ks)
    o = eg * _bmm(pad(q), s, precision=HIGHEST)[:, 0, :] + jnp.sum(q * k, axis=-1, keepdims=True) * v_new
    s_ref[...] = s * eg[:, :, None] + _bmm_tn(pad(k), pad(v_new), precision=HIGHEST)
    z = _head_rows(gz_ref[pl.ds(b, 1), :], nh)
    o = _rms(o) * g_ref[...] * (z * _sigmoid(z))
    o_ref[pl.ds(b, 1), :] = jnp.concatenate([o[h:h + 1] for h in range(nh)], axis=1)


def _gdn_sample(gz, hist, conv_w, small, a_log_row, dt_row, gdn_g, s0, prev, row0, a_col, b_col):
    nd = hist.shape[0]
    rb = row0 // SAMPLE_PAD
    full = lambda shp: pl.BlockSpec(shp, lambda b: (0,) * len(shp))
    return pl.pallas_call(
        functools.partial(_gdn_sample_kernel, a_col=a_col, b_col=b_col),
        grid=(nd,),
        in_specs=[pl.BlockSpec((SAMPLE_PAD, 3 * GH), lambda b: (rb, 0)),
                  pl.BlockSpec((None, GDN_CONV - 1, 3 * GH), lambda b: (b, 0, 0)),
                  full(conv_w.shape),
                  pl.BlockSpec((SAMPLE_PAD, GH), lambda b: (rb, 3)),
                  pl.BlockSpec((SAMPLE_PAD, small.shape[1]), lambda b: (rb, 0)),
                  full(a_log_row.shape), full(dt_row.shape), full((1, HEAD_DIM)),
                  pl.BlockSpec((None, GDN_HEADS, HEAD_DIM, HEAD_DIM), lambda b: (b, 0, 0, 0)),
                  pl.BlockSpec(memory_space=pl.ANY)],
        out_specs=[pl.BlockSpec((SAMPLE_PAD, GH), lambda b: (rb, 0)),
                   pl.BlockSpec((None, GDN_HEADS, HEAD_DIM, HEAD_DIM), lambda b: (b, 0, 0, 0))],
        out_shape=[jax.ShapeDtypeStruct(prev.shape, F32), jax.ShapeDtypeStruct(s0.shape, F32)],
        input_output_aliases={9: 0},
        compiler_params=_cparams(("arbitrary",)), name="gdn_sample",
    )(gz, hist, conv_w, gz, small, a_log_row, dt_row, gdn_g.reshape(1, HEAD_DIM), s0, prev)


def _even_layer(hn, e, dims, slopes, cache_nsa_kv, cache_nsa_win, state_gdn, state_gdn_conv, page_table,
                w_in, cmp_w, conv_w, a_log, dt_bias, gdn_g, w_out):
    batch, seq, nd, d = dims
    rows_p = batch * seq
    mp = hn.shape[0]
    past = page_table.shape[1] * cache_nsa_kv.shape[2]
    c_nsa = QW + 6 * KVW
    c_gate = c_nsa
    c_gdn = c_gate + 3 * NSA_HEADS
    c_a = c_gdn + 4 * GH
    a_col, b_col = 3 * NSA_HEADS, 3 * NSA_HEADS + GDN_HEADS
    hd = HEAD_DIM

    proj = _matmul([hn], [(w_in, d, 0, 0)], (0,), _ep_plain, c_nsa, F32, _col_tile(c_nsa, 512), "proj_nsa")
    w_gdn = lax.slice_in_dim(w_in, c_gdn, c_a, axis=1)
    gz = _matmul([hn], [(w_gdn, d, 0, 0)], (0,), _ep_plain, 4 * GH, F32, _col_tile(4 * GH, 512), "proj_gdn")
    w_small = jnp.concatenate([lax.slice_in_dim(w_in, c_gate, c_gdn, axis=1), lax.slice_in_dim(w_in, c_a, w_in.shape[1], axis=1),
                               jnp.zeros((d, hd - 3 * NSA_HEADS - 2 * GDN_HEADS), F32)], axis=1)
    small = _matmul([hn], [(w_small, d, 0, 0)], (0,), _ep_plain, hd, F32, hd, "proj_small")

    w_exp = jnp.concatenate([jnp.repeat(cmp_w[:, :, 0], hd, axis=1), jnp.repeat(cmp_w[:, :, 1], hd, axis=1)], axis=1)
    kcvc = _pool_prompt(proj, w_exp, batch, seq)
    gates_t = small[:, 0:3 * NSA_HEADS].reshape(mp, NSA_KV_HEADS, 3 * NSA_GROUP).transpose(1, 0, 2)
    o_nsa = _nsa_prompt(proj, kcvc, gates_t, slopes, batch, seq)
    wk = jnp.broadcast_to(cmp_w[:, :, 0][:, :, None], (NSA_BLOCK, NSA_KV_HEADS, hd))
    wv = jnp.broadcast_to(cmp_w[:, :, 1][:, :, None], (NSA_BLOCK, NSA_KV_HEADS, hd))
    kcs, vcs = _pool_sample(cache_nsa_kv, e, page_table, wk, wv)
    o_cmp, sel_rows = _nsa_cmp_sample(proj, kcs, vcs, slopes, rows_p, past)
    idx = sel_rows[:, :, :, 0].reshape(-1)
    o_nsa = _nsa_sel_sample(proj, small, cache_nsa_kv, cache_nsa_win, e, page_table, idx, o_cmp, slopes, o_nsa,
                            rows_p, past)

    pad_row = lambda vec, col: jnp.zeros((1, hd), F32).at[0, col:col + GDN_HEADS].set(vec.astype(F32))
    a_log_row, dt_row = pad_row(a_log, a_col), pad_row(dt_bias, a_col)
    qkv_c = _dwconv_prompt(gz, conv_w, jnp.zeros((3 * GH,), F32), 0, 3 * GH, batch, seq, True, "gdn_conv_prompt")
    intra = _gdn_intra(qkv_c, small, a_log_row, dt_row, rows_p, a_col, b_col)
    o_gdn, s_prompt = _gdn_scan(intra, gz, gdn_g, mp, batch, seq)
    o_gdn, s_sample = _gdn_sample(gz, state_gdn_conv[e], conv_w, small, a_log_row, dt_row, gdn_g, state_gdn[e], o_gdn,
                                  rows_p, a_col, b_col)

    mixed = _matmul([o_nsa.astype(BF16), o_gdn.astype(BF16)], [(w_out, QW, 0, 0), (w_out, GH, 1, 0)], (0, 1), _ep_sum,
                    d, F32, _col_tile(d, 512), "proj_out")

    kv_rows = proj[:, QW:QW + 4 * KVW]
    win_rows = proj[:, QW + 4 * KVW:QW + 6 * KVW]
    kv_p = kv_rows[:rows_p].reshape(batch, seq, 4, NSA_KV_HEADS, hd)
    kv_s = kv_rows[rows_p:rows_p + nd].reshape(nd, 1, 4, NSA_KV_HEADS, hd)
    nwin_p = min(NSA_WINDOW, seq)
    win_p = win_rows[:rows_p].reshape(batch, seq, 2, NSA_KV_HEADS, hd)[:, seq - nwin_p:]
    win_all = jnp.concatenate([cache_nsa_win[e], win_rows[rows_p:rows_p + nd].reshape(nd, 1, 2, NSA_KV_HEADS, hd)], axis=1)
    win_s = win_all[:, win_all.shape[1] - min(NSA_WINDOW, win_all.shape[1]):]
    qkv_g = gz[:, 0:3 * GH]
    gconv_p = qkv_g[:rows_p].reshape(batch, seq, 3 * GH)[:, seq - (GDN_CONV - 1):]
    gconv_s = jnp.concatenate([state_gdn_conv[e], qkv_g[rows_p:rows_p + nd][:, None]], axis=1)[:, 1:]
    return mixed, (kv_p, kv_s, win_p, win_s, s_prompt, s_sample, gconv_p, gconv_s)


def _odd_layer(hn, o, dims, state_conf_conv, w_pw1, w_dw, b_dw, ln_g, ln_b, w_pw2):
    batch, seq, nd, d = dims
    rows_p = batch * seq
    ch = w_dw.shape[1]
    tn = _col_tile(ch, 512)
    u = _matmul([hn], [(w_pw1, d, 0, 0), (w_pw1, d, 0, ch // tn)], (0, 0), _ep_glu, ch, F32, tn, "conf_pw1_glu")
    c = _dwconv_prompt(u, w_dw, b_dw, 0, ch, batch, seq, False, "conf_conv_prompt")
    c = _dwconv_sample(u, state_conf_conv[o], w_dw, b_dw, 0, ch, c, rows_p, False, "conf_conv_sample")
    y = _ln_silu(c, ln_g, ln_b)
    mixed = _matmul([y], [(w_pw2, ch, 0, 0)], (0,), _ep_plain, d, F32, _col_tile(d, 512), "conf_pw2")
    cc_p = u[:rows_p].reshape(batch, seq, ch)[:, seq - (CONF_WIDTH - 1):]
    cc_s = jnp.concatenate([state_conf_conv[o], u[rows_p:rows_p + nd][:, None]], axis=1)[:, 1:]
    return mixed, (cc_p, cc_s)


def kernel(x_prompt, x_sample, cache_nsa_kv, cache_nsa_win, state_gdn, state_gdn_conv, state_conf_conv, page_table,
           p_prompt, p_sample, norm_g, w_in_even, cmp_pool_w, gdn_conv_w, gdn_a_log, gdn_dt_bias, gdn_norm_g,
           w_out_even, w_pw1, w_dw, b_dw, conf_ln_g, conf_ln_b, w_pw2, w_ffn_gate, w_ffn_up, w_ffn_down,
           w_ple_gate, w_ple_proj):
    batch, seq, d = x_prompt.shape
    nd = x_sample.shape[0]
    depth = norm_g.shape[0]
    ffn = w_ffn_gate.shape[2]
    ple = w_ple_proj.shape[1]
    rows_p = batch * seq
    assert x_sample.shape[1] == 1 and nd <= 8 and rows_p % SAMPLE_PAD == 0
    dims = (batch, seq, nd, d)
    slopes = jnp.asarray(np.asarray([2.0 ** (-8.0 * (h + 1) / NSA_HEADS) for h in range(NSA_HEADS)], np.float32)
                         .reshape(NSA_KV_HEADS, NSA_GROUP))

    padrows = lambda a: jnp.concatenate([a, jnp.zeros((SAMPLE_PAD - nd,) + a.shape[1:], a.dtype)], axis=0)
    h = jnp.concatenate([x_prompt.reshape(rows_p, d), padrows(x_sample.reshape(nd, d))], axis=0)
    ones = jnp.ones((d,), F32)
    tn_f = _col_tile(ffn, 256)
    outs = [[] for _ in range(10)]
    hn = _norm(h, norm_g[0, 0])
    for i in range(depth):
        g_post, f_pre, f_post = norm_g[i, 1], norm_g[i, 2], norm_g[i, 3]
        if i % 2 == 0:
            e = i // 2
            mixed, st = _even_layer(hn, e, dims, slopes, cache_nsa_kv, cache_nsa_win, state_gdn, state_gdn_conv,
                                    page_table, w_in_even[e], cmp_pool_w[e], gdn_conv_w[e], gdn_a_log[e],
                                    gdn_dt_bias[e], gdn_norm_g[e], w_out_even[e])
            for lst, val in zip(outs[:8], st):
                lst.append(val)
        else:
            o = i // 2
            mixed, st = _odd_layer(hn, o, dims, state_conf_conv, w_pw1[o], w_dw[o], b_dw[o], conf_ln_g[o],
                                   conf_ln_b[o], w_pw2[o])
            outs[8].append(st[0])
            outs[9].append(st[1])
        h, hn = _addnorm(h, mixed, g_post, f_pre)
        act = _matmul([hn], [(w_ffn_gate[i], d, 0, 0), (w_ffn_up[i], d, 0, 0)], (0, 0), _ep_swiglu, ffn, BF16, tn_f,
                      "ffn_gate_up")
        f = _matmul([act], [(w_ffn_down[i], ffn, 0, 0)], (0,), _ep_plain, d, F32, _col_tile(d, 256), "ffn_down")
        h, hn = _addnorm(h, f, f_post, ones)
        p_i = jnp.concatenate([p_prompt[i].reshape(rows_p, ple), padrows(p_sample[i].reshape(nd, ple))], axis=0).astype(BF16)
        h = _matmul([hn, p_i], [(w_ple_gate[i], d, 0, 0), (w_ple_proj[i], ple, 0, 0)], (0, 1), _ep_ple, d, F32,
                    _col_tile(d, 512), "ple", res=h)
        if i + 1 < depth:
            hn = _norm(h, norm_g[i + 1, 0])
    y_p = h[:rows_p].reshape(batch, seq, d)
    y_s = h[rows_p:rows_p + nd].reshape(nd, 1, d)
    return (y_p, y_s) + tuple(jnp.stack(lst) for lst in outs)
```

```python
import functools
import math

import numpy as np
import jax
import jax.numpy as jnp
from jax import lax
from jax.experimental import pallas as pl
from jax.experimental.pallas import tpu as pltpu

F32 = jnp.float32
BF16 = jnp.bfloat16
I32 = jnp.int32
HIGHEST = lax.Precision.HIGHEST

HEAD_DIM = 128
NSA_HEADS = 16
NSA_KV_HEADS = 4
NSA_GROUP = NSA_HEADS // NSA_KV_HEADS
NSA_BLOCK = 64
NSA_TOPK = 16
NSA_WINDOW = 512
NSA_FORCE = 1.0e4
GDN_HEADS = 16
GDN_CONV = 4
GDN_CHUNK = 64
CONF_WIDTH = 31
EPS = 1e-6
NEG = -1.0e30

SAMPLE_PAD = 16
V7X_VMEM_LIMIT = 58 * 1024 * 1024
MM_VMEM_BUDGET = 50 * 1024 * 1024
MM_MAX_ROWS = 1024
NSA_QTILE = 128
SEL_KCHUNK = 512

QW = NSA_HEADS * HEAD_DIM
GW = NSA_GROUP * HEAD_DIM
KVW = NSA_KV_HEADS * HEAD_DIM
GH = GDN_HEADS * HEAD_DIM


def _cparams(sem):
    return pltpu.CompilerParams(dimension_semantics=sem, vmem_limit_bytes=V7X_VMEM_LIMIT)


def _divisor_tile(n, target, mult):
    best = None
    for t in range(mult, min(n, target) + 1, mult):
        if n % t == 0:
            best = t
    assert best is not None, (n, target, mult)
    return best


def _sigmoid(x):
    return 1.0 / (1.0 + jnp.exp(-x))


def _iota(shape, dim):
    return lax.broadcasted_iota(I32, shape, dim)


def _rms(x):
    return x * lax.rsqrt(jnp.mean(x * x, axis=-1, keepdims=True) + EPS)


def _msoftmax(s, mask):
    s = jnp.where(mask, s, NEG)
    m = jnp.max(s, axis=-1, keepdims=True)
    e = jnp.where(mask, jnp.exp(s - m), 0.0)
    return e / jnp.maximum(jnp.sum(e, axis=-1, keepdims=True), 1e-30)


def _dot_nt(a, b, **kw):
    return lax.dot_general(a, b, (((1,), (1,)), ((), ())), preferred_element_type=F32, **kw)


def _norm_kernel(h_ref, g_ref, o_ref):
    o_ref[...] = (_rms(h_ref[...]) * g_ref[...]).astype(o_ref.dtype)


def _addnorm_kernel(h_ref, f_ref, g1_ref, g2_ref, h2_ref, hn_ref):
    h2 = h_ref[...] + _rms(f_ref[...]) * g1_ref[...]
    h2_ref[...] = h2
    hn_ref[...] = (_rms(h2) * g2_ref[...]).astype(hn_ref.dtype)


def _ln_silu_kernel(c_ref, g_ref, b_ref, o_ref):
    x = c_ref[...]
    xc = x - jnp.mean(x, axis=-1, keepdims=True)
    y = xc * lax.rsqrt(jnp.mean(xc * xc, axis=-1, keepdims=True) + EPS)
    y = y * g_ref[...] + b_ref[...]
    o_ref[...] = (y * _sigmoid(y)).astype(o_ref.dtype)


def _row_call(kernel, mats, vecs, out_dtypes, name):
    mp, d = mats[0].shape
    tr = _divisor_tile(mp, 160, 16)
    mat_spec = pl.BlockSpec((tr, d), lambda i: (i, 0))
    vec_spec = pl.BlockSpec((1, d), lambda i: (0, 0))
    outs = [jax.ShapeDtypeStruct((mp, d), dt) for dt in out_dtypes]
    res = pl.pallas_call(
        kernel, grid=(mp // tr,),
        in_specs=[mat_spec] * len(mats) + [vec_spec] * len(vecs),
        out_specs=[mat_spec] * len(outs), out_shape=outs,
        compiler_params=_cparams(("parallel",)), name=name,
    )(*mats, *[v.reshape(1, d).astype(F32) for v in vecs])
    return res


def _norm(h, g):
    return _row_call(_norm_kernel, [h], [g], [BF16], "rmsnorm")[0]


def _addnorm(h, f, g1, g2):
    return _row_call(_addnorm_kernel, [h, f], [g1, g2], [F32, BF16], "add_rmsnorm")


def _ln_silu(c, g, b):
    return _row_call(_ln_silu_kernel, [c], [g, b], [BF16], "layernorm_silu")[0]


def _ep_plain(accs, res):
    return accs[0]


def _ep_sum(accs, res):
    return accs[0] + accs[1]


def _ep_swiglu(accs, res):
    return accs[0] * _sigmoid(accs[0]) * accs[1]


def _ep_glu(accs, res):
    return accs[0] * _sigmoid(accs[1])


def _ep_ple(accs, res):
    return res + _sigmoid(accs[0]) * accs[1]


def _mm_kernel(*refs, x_of_w, epilogue, has_res, cast):
    nx = max(x_of_w) + 1
    nw = len(x_of_w)
    x_refs, w_refs = refs[:nx], refs[nx:nx + nw]
    pos = nx + nw
    res_ref = refs[pos] if has_res else None
    pos += int(has_res)
    o_ref = refs[pos]
    wb_refs = refs[pos + 1:] if cast else w_refs

    if cast:
        @pl.when(pl.program_id(1) == 0)
        def _():
            for w_ref, wb_ref in zip(w_refs, wb_refs):
                wb_ref[...] = w_ref[...].astype(BF16)

    accs = [jnp.dot(x_refs[xi][...], wb[...], preferred_element_type=F32) for xi, wb in zip(x_of_w, wb_refs)]
    res = res_ref[...] if has_res else None
    o_ref[...] = epilogue(accs, res).astype(o_ref.dtype)


def _mm_row_tile(mp, xs, ws, tn, out_dtype, has_res, x_stationary):
    fixed = 0
    for (w, _, k, _, _) in ws:
        fixed += 2 * k * tn * w.dtype.itemsize
        fixed += k * tn * 2 if w.dtype != BF16 else 0
    per_row = sum(x.shape[1] * x.dtype.itemsize * (1 if x_stationary else 2) for x in xs)
    per_row += 2 * tn * jnp.dtype(out_dtype).itemsize + (2 * tn * 4 if has_res else 0)
    per_row += len(ws) * tn * 4
    best = SAMPLE_PAD
    for tm in range(SAMPLE_PAD, MM_MAX_ROWS + 1, SAMPLE_PAD):
        if mp % tm == 0 and fixed + tm * per_row <= MM_VMEM_BUDGET:
            best = tm
    return best


def _matmul(xs, ws, x_of_w, epilogue, n_out, out_dtype, tn, name, res=None, x_stationary=False):
    mp = xs[0].shape[0]
    assert n_out % tn == 0
    cast = not x_stationary
    assert all((w.dtype == BF16) == x_stationary for (w, _, _, _, _) in ws)
    tm = _mm_row_tile(mp, xs, ws, tn, out_dtype, res is not None, x_stationary)
    if x_stationary:
        grid = (mp // tm, n_out // tn)
        ij = lambda a, b: (a, b)
    else:
        grid = (n_out // tn, mp // tm)
        ij = lambda a, b: (b, a)
    x_mode = dict(pipeline_mode=pl.Buffered(1)) if x_stationary else {}
    in_specs = [pl.BlockSpec((tm, x.shape[1]), lambda a, b: (ij(a, b)[0], 0), **x_mode) for x in xs]
    arrs = list(xs)
    scratch = []
    for (w, layer, k, rb, cb) in ws:
        in_specs.append(pl.BlockSpec((None, k, tn), functools.partial(
            lambda a, b, layer, rb, cb: (layer, rb, cb + ij(a, b)[1]), layer=layer, rb=rb, cb=cb)))
        arrs.append(w)
        if cast:
            scratch.append(pltpu.VMEM((k, tn), BF16))
    if res is not None:
        in_specs.append(pl.BlockSpec((tm, tn), lambda a, b: ij(a, b)))
        arrs.append(res)
    return pl.pallas_call(
        functools.partial(_mm_kernel, x_of_w=tuple(x_of_w), epilogue=epilogue, has_res=res is not None, cast=cast),
        grid=grid,
        in_specs=in_specs,
        out_specs=pl.BlockSpec((tm, tn), lambda a, b: ij(a, b)),
        out_shape=jax.ShapeDtypeStruct((mp, n_out), out_dtype),
        scratch_shapes=scratch,
        compiler_params=_cparams(("arbitrary", "arbitrary")), name=name,
    )(*arrs)


def _col_tile(n, target):
    return _divisor_tile(n, target, 128)


def _dwconv_prompt_kernel(x_ref, w_ref, b_ref, o_ref, xs_ref, *, width, seq, pad, rows, act):
    ct = x_ref.shape[1]
    xs_ref[0:pad, :] = jnp.zeros((pad, ct), F32)
    xs_ref[pad:pad + seq, :] = x_ref[...]
    bias = b_ref[...]

    def body(i, carry):
        r0 = pl.multiple_of(i * rows, rows)
        acc = jnp.zeros((rows, ct), F32) + bias
        win = xs_ref[pl.ds(r0, rows + pad), :]
        off = pad - (width - 1)
        for r in range(8):
            taps = [j for j in range(width) if (off + j) % 8 == r]
            if not taps:
                continue
            wr = win if r == 0 else pltpu.roll(win, rows + pad - r, 0)
            for j in taps:
                base = off + j - r
                acc = acc + w_ref[pl.ds(j, 1), :] * wr[base:base + rows]
        if act:
            acc = acc * _sigmoid(acc)
        o_ref[pl.ds(r0, rows), :] = acc
        return carry

    lax.fori_loop(0, seq // rows, body, 0)


def _dwconv_prompt(x, w, bias, col0, ncols, batch, seq, act, name):
    mp = x.shape[0]
    width = w.shape[0]
    ct = _col_tile(ncols, 512)
    assert col0 % ct == 0
    pad = -(-(width - 1) // 8) * 8
    rows = 32
    return pl.pallas_call(
        functools.partial(_dwconv_prompt_kernel, width=width, seq=seq, pad=pad, rows=rows, act=act),
        grid=(batch, ncols // ct),
        in_specs=[pl.BlockSpec((seq, ct), lambda b, j: (b, col0 // ct + j)),
                  pl.BlockSpec((width, ct), lambda b, j: (0, j)),
                  pl.BlockSpec((1, ct), lambda b, j: (0, j))],
        out_specs=pl.BlockSpec((seq, ct), lambda b, j: (b, j)),
        out_shape=jax.ShapeDtypeStruct((mp, ncols), F32),
        scratch_shapes=[pltpu.VMEM((pad + seq, ct), F32)],
        compiler_params=_cparams(("parallel", "parallel")), name=name,
    )(x, w, bias.reshape(1, ncols))


def _dwconv_sample_kernel(x_ref, h_ref, w_ref, b_ref, prev_ref, o_ref, *, width, nb, act):
    del prev_ref
    ct = x_ref.shape[1]
    acc = jnp.sum(h_ref[...] * w_ref[0:width - 1, :][None], axis=1)
    acc = acc + x_ref[0:nb, :] * w_ref[width - 1:width, :] + b_ref[...]
    if act:
        acc = acc * _sigmoid(acc)
    o_ref[0:nb, :] = acc
    o_ref[nb:, :] = jnp.zeros((SAMPLE_PAD - nb, ct), F32)


def _dwconv_sample(x, hist, w, bias, col0, ncols, prev, row0, act, name):
    nb = hist.shape[0]
    width = w.shape[0]
    ct = _col_tile(ncols, 512)
    rb = row0 // SAMPLE_PAD
    return pl.pallas_call(
        functools.partial(_dwconv_sample_kernel, width=width, nb=nb, act=act),
        grid=(ncols // ct,),
        in_specs=[pl.BlockSpec((SAMPLE_PAD, ct), lambda j: (rb, col0 // ct + j)),
                  pl.BlockSpec((nb, width - 1, ct), lambda j: (0, 0, j)),
                  pl.BlockSpec((width, ct), lambda j: (0, j)),
                  pl.BlockSpec((1, ct), lambda j: (0, j)),
                  pl.BlockSpec(memory_space=pl.ANY)],
        out_specs=pl.BlockSpec((SAMPLE_PAD, ct), lambda j: (rb, j)),
        out_shape=jax.ShapeDtypeStruct(prev.shape, F32),
        input_output_aliases={4: 0},
        compiler_params=_cparams(("parallel",)), name=name,
    )(x, hist, w, bias.reshape(1, ncols), prev)


def _pool_prompt_kernel(x_ref, w_ref, o_ref, *, nblk):
    x = x_ref[...].reshape(nblk, NSA_BLOCK, x_ref.shape[1])
    o_ref[...] = jnp.sum(x * w_ref[...][None], axis=1)


def _pool_prompt(proj, w_exp, batch, seq):
    nblk = seq // NSA_BLOCK
    return pl.pallas_call(
        functools.partial(_pool_prompt_kernel, nblk=nblk),
        grid=(batch,),
        in_specs=[pl.BlockSpec((seq, 2 * KVW), lambda b: (b, QW // (2 * KVW))),
                  pl.BlockSpec((NSA_BLOCK, 2 * KVW), lambda b: (0, 0))],
        out_specs=pl.BlockSpec((None, nblk, 2 * KVW), lambda b: (b, 0, 0)),
        out_shape=jax.ShapeDtypeStruct((batch, nblk, 2 * KVW), F32),
        compiler_params=_cparams(("parallel",)), name="nsa_pool_prompt",
    )(proj, w_exp)


def _nsa_prompt_kernel(sl_ref, q_ref, kc_ref, vc_ref, ks_ref, vs_ref, kw_ref, vw_ref, gt_ref, o_ref, *, tq, seq):
    g = pl.program_id(1)
    p0 = pl.program_id(2) * tq
    nblk = seq // NSA_BLOCK
    n_keep = min(NSA_TOPK, nblk)
    scale = HEAD_DIM ** -0.5
    rr = NSA_GROUP

    q = q_ref[...]
    qs = jnp.concatenate([q[:, r * HEAD_DIM:(r + 1) * HEAD_DIM] for r in range(rr)], axis=0).astype(BF16)
    pos = p0 + _iota((tq, 1), 0)
    pos4 = jnp.concatenate([pos] * rr, axis=0)
    slope4 = jnp.concatenate([jnp.full((tq, 1), sl_ref[g, r], F32) for r in range(rr)], axis=0)

    kc = kc_ref[...].astype(BF16)
    vc = vc_ref[...].astype(BF16)
    blk = _iota((1, nblk), 1)
    blk_end = blk * NSA_BLOCK + (NSA_BLOCK - 1)
    s = _dot_nt(qs, kc) * scale - slope4 * (pos4 - blk_end).astype(F32)
    p = _msoftmax(s, blk_end <= pos4)
    o_c = jnp.dot(p.astype(BF16), vc, preferred_element_type=F32)

    imp = p[0:tq]
    for r in range(1, rr):
        imp = imp + p[r * tq:(r + 1) * tq]
    cur = pos // NSA_BLOCK
    forced = (blk == 0) | (blk == cur) | (blk == cur - 1)
    imp = jnp.where(forced, NSA_FORCE, imp)
    imp = jnp.where(blk * NSA_BLOCK > pos, -1.0, imp)
    rank = jnp.zeros((tq, nblk), I32)
    for i in range(nblk):
        c = imp[:, i:i + 1]
        rank = rank + ((c > imp) | ((c == imp) & (blk > i))).astype(I32)
    sel_f = (rank < n_keep).astype(BF16)

    kch = min(SEL_KCHUNK, seq)
    n_chunks = (p0 + tq + kch - 1) // kch

    def sel_body(c, carry):
        m, l, acc = carry
        k0 = pl.multiple_of(c * kch, kch)
        kk = ks_ref[pl.ds(k0, kch), :].astype(BF16)
        vv = vs_ref[pl.ds(k0, kch), :].astype(BF16)
        kpos = k0 + _iota((1, kch), 1)
        expand = (_iota((nblk, kch), 0) == ((k0 + _iota((nblk, kch), 1)) // NSA_BLOCK)).astype(BF16)
        selk = jnp.dot(sel_f, expand, preferred_element_type=F32) > 0.5
        mask1 = selk & (kpos <= pos)
        mask = jnp.concatenate([mask1] * rr, axis=0)
        sc = _dot_nt(qs, kk) * scale - slope4 * (pos4 - kpos).astype(F32)
        sc = jnp.where(mask, sc, NEG)
        m_new = jnp.maximum(m, jnp.max(sc, axis=-1, keepdims=True))
        alpha = jnp.exp(m - m_new)
        e = jnp.exp(sc - m_new)
        l = alpha * l + jnp.sum(e, axis=-1, keepdims=True)
        acc = alpha * acc + jnp.dot(e.astype(BF16), vv, preferred_element_type=F32)
        return m_new, l, acc

    init = (jnp.full((rr * tq, 1), NEG, F32), jnp.zeros((rr * tq, 1), F32), jnp.zeros((rr * tq, HEAD_DIM), F32))
    _, l, acc = lax.fori_loop(0, n_chunks, sel_body, init)
    o_s = acc / jnp.maximum(l, 1e-30)

    span = min(NSA_WINDOW + tq, seq)
    w0 = pl.multiple_of(jnp.clip(p0 + tq - span, 0, seq - span), 128)
    kk = kw_ref[pl.ds(w0, span), :].astype(BF16)
    vv = vw_ref[pl.ds(w0, span), :].astype(BF16)
    dist = pos4 - (w0 + _iota((1, span), 1))
    sw = _dot_nt(qs, kk) * scale - slope4 * dist.astype(F32)
    pw = _msoftmax(sw, (dist >= 0) & (dist < NSA_WINDOW))
    o_w = jnp.dot(pw.astype(BF16), vv, preferred_element_type=F32)

    gt = _sigmoid(gt_ref[...])
    for r in range(rr):
        rows = slice(r * tq, (r + 1) * tq)
        o_ref[:, r * HEAD_DIM:(r + 1) * HEAD_DIM] = (gt[:, 3 * r:3 * r + 1] * o_c[rows]
                                                    + gt[:, 3 * r + 1:3 * r + 2] * o_s[rows]
                                                    + gt[:, 3 * r + 2:3 * r + 3] * o_w[rows])


def _nsa_prompt(proj, kcvc, gates_t, slopes, batch, seq):
    mp = proj.shape[0]
    tq = min(NSA_QTILE, seq)
    nq = seq // tq
    nblk = seq // NSA_BLOCK
    hb = HEAD_DIM
    c_sel = (QW + 2 * KVW) // hb
    c_win = (QW + 4 * KVW) // hb
    g4 = NSA_KV_HEADS
    return pl.pallas_call(
        functools.partial(_nsa_prompt_kernel, tq=tq, seq=seq),
        grid=(batch, NSA_KV_HEADS, nq),
        in_specs=[pl.BlockSpec(memory_space=pltpu.SMEM),
                  pl.BlockSpec((tq, GW), lambda b, g, i: (b * nq + i, g)),
                  pl.BlockSpec((None, nblk, hb), lambda b, g, i: (b, 0, g)),
                  pl.BlockSpec((None, nblk, hb), lambda b, g, i: (b, 0, g4 + g)),
                  pl.BlockSpec((seq, hb), lambda b, g, i: (b, c_sel + g)),
                  pl.BlockSpec((seq, hb), lambda b, g, i: (b, c_sel + g4 + g)),
                  pl.BlockSpec((seq, hb), lambda b, g, i: (b, c_win + g)),
                  pl.BlockSpec((seq, hb), lambda b, g, i: (b, c_win + g4 + g)),
                  pl.BlockSpec((None, tq, 3 * NSA_GROUP), lambda b, g, i: (g, b * nq + i, 0))],
        out_specs=pl.BlockSpec((tq, GW), lambda b, g, i: (b * nq + i, g)),
        out_shape=jax.ShapeDtypeStruct((mp, QW), F32),
        compiler_params=_cparams(("parallel", "parallel", "arbitrary")), name="nsa_prompt",
    )(slopes, proj, kcvc, kcvc, proj, proj, proj, proj, gates_t)


def _pool_sample_kernel(pt_ref, c_ref, wk_ref, wv_ref, ko_ref, vo_ref):
    del pt_ref
    for half in range(c_ref.shape[0] // NSA_BLOCK):
        rows = pl.ds(half * NSA_BLOCK, NSA_BLOCK)
        ko_ref[half] = jnp.sum(c_ref[rows, 0] * wk_ref[...], axis=0)
        vo_ref[half] = jnp.sum(c_ref[rows, 1] * wv_ref[...], axis=0)


def _pool_sample(cache, e, page_table, wk, wv):
    nd, n_pages = page_table.shape
    page_rows = cache.shape[2]
    per = page_rows // NSA_BLOCK
    kvh, hd = cache.shape[4], cache.shape[5]
    out = jax.ShapeDtypeStruct((nd, n_pages * per, kvh, hd), F32)
    grid_spec = pltpu.PrefetchScalarGridSpec(
        num_scalar_prefetch=1, grid=(nd, n_pages),
        in_specs=[pl.BlockSpec((None, None, page_rows, 2, kvh, hd), lambda b, p, pt: (e, pt[b, p], 0, 0, 0, 0)),
                  pl.BlockSpec((NSA_BLOCK, kvh, hd), lambda b, p, pt: (0, 0, 0)),
                  pl.BlockSpec((NSA_BLOCK, kvh, hd), lambda b, p, pt: (0, 0, 0))],
        out_specs=[pl.BlockSpec((None, per, kvh, hd), lambda b, p, pt: (b, p, 0, 0))] * 2)
    return pl.pallas_call(_pool_sample_kernel, grid_spec=grid_spec, out_shape=[out, out],
                          compiler_params=_cparams(("parallel", "parallel")), name="nsa_pool_sample",
                          )(page_table, cache, wk, wv)


def _head_rows(row, n):
    return jnp.concatenate([row[:, h * HEAD_DIM:(h + 1) * HEAD_DIM] for h in range(n)], axis=0)


def _nsa_cmp_sample_kernel(sl_ref, q_ref, kc_ref, vc_ref, oc_ref, oh_ref, *, past, n_sel):
    b = pl.program_id(0)
    nblk = kc_ref.shape[0]
    scale = HEAD_DIM ** -0.5
    rr = NSA_GROUP
    qh = _head_rows(q_ref[pl.ds(b, 1), :], NSA_HEADS).astype(BF16)
    blk = _iota((1, nblk), 1)
    dist = (past - (blk * NSA_BLOCK + (NSA_BLOCK - 1))).astype(F32)
    last = (past // NSA_BLOCK) - 1
    for g in range(NSA_KV_HEADS):
        qg = jnp.concatenate([qh[g * rr:(g + 1) * rr], jnp.zeros((8 - rr, HEAD_DIM), BF16)], axis=0)
        slope = jnp.concatenate([jnp.full((1, 1), sl_ref[g, r], F32) for r in range(rr)]
                                + [jnp.zeros((8 - rr, 1), F32)], axis=0)
        kc = kc_ref[:, g, :].astype(BF16)
        vc = vc_ref[:, g, :].astype(BF16)
        s = _dot_nt(qg, kc) * scale - slope * dist
        p = _msoftmax(s, blk >= 0)
        oc_ref[g * 8:(g + 1) * 8, :] = jnp.dot(p.astype(BF16), vc, preferred_element_type=F32)
        imp = jnp.sum(p[0:rr], axis=0, keepdims=True)
        imp = jnp.where((blk == 0) | (blk == last), NSA_FORCE, imp)
        impb = jnp.broadcast_to(imp, (nblk, nblk))
        eye = _iota((nblk, nblk), 0) == _iota((nblk, nblk), 1)
        impc = jnp.sum(jnp.where(eye, impb, 0.0), axis=1, keepdims=True)
        before = (impc > impb) | ((impc == impb) & (_iota((nblk, nblk), 0) < _iota((nblk, nblk), 1)))
        rank = jnp.sum(before.astype(F32), axis=0, keepdims=True)
        sel = rank < n_sel
        lower = (_iota((nblk, nblk), 0) < _iota((nblk, nblk), 1)).astype(BF16)
        slot = jnp.dot(jnp.broadcast_to(sel.astype(BF16), (8, nblk)), lower, preferred_element_type=F32)[0:1]
        onehot = sel & (jnp.abs(slot - _iota((NSA_TOPK, nblk), 0).astype(F32)) < 0.5)
        idx = jnp.sum(jnp.where(onehot, _iota((NSA_TOPK, nblk), 1), 0), axis=1, keepdims=True)
        oh_ref[g] = jnp.broadcast_to(idx, (NSA_TOPK, HEAD_DIM))


def _nsa_cmp_sample(proj, kcs, vcs, slopes, row0, past):
    nd, nblk = kcs.shape[0], kcs.shape[1]
    n_sel = min(NSA_TOPK, nblk + 1) - 1
    assert past % NSA_BLOCK == 0 and nblk >= NSA_TOPK
    rb = row0 // SAMPLE_PAD
    return pl.pallas_call(
        functools.partial(_nsa_cmp_sample_kernel, past=past, n_sel=n_sel),
        grid=(nd,),
        in_specs=[pl.BlockSpec(memory_space=pltpu.SMEM),
                  pl.BlockSpec((SAMPLE_PAD, QW), lambda b: (rb, 0)),
                  pl.BlockSpec((None, nblk, NSA_KV_HEADS, HEAD_DIM), lambda b: (b, 0, 0, 0)),
                  pl.BlockSpec((None, nblk, NSA_KV_HEADS, HEAD_DIM), lambda b: (b, 0, 0, 0))],
        out_specs=[pl.BlockSpec((None, 8 * NSA_KV_HEADS, HEAD_DIM), lambda b: (b, 0, 0)),
                   pl.BlockSpec((None, NSA_KV_HEADS, NSA_TOPK, HEAD_DIM), lambda b: (b, 0, 0, 0))],
        out_shape=[jax.ShapeDtypeStruct((nd, 8 * NSA_KV_HEADS, HEAD_DIM), F32),
                   jax.ShapeDtypeStruct((nd, NSA_KV_HEADS, NSA_TOPK, HEAD_DIM), I32)],
        compiler_params=_cparams(("parallel",)), name="nsa_cmp_sample",
    )(slopes, proj, kcs, vcs)


def _nsa_sel_sample_kernel(idx_ref, pt_ref, sl_ref, proj_ref, c0_ref, c1_ref, c2_ref, c3_ref, win_ref, oc_ref,
                           gt_ref, prev_ref, o_ref, m_ref, l_ref, acc_ref, *, past, n_sel, nd):
    del pt_ref, prev_ref
    b = pl.program_id(0)
    j = pl.program_id(1)
    scale = HEAD_DIM ** -0.5
    rr = NSA_GROUP
    c_refs = (c0_ref, c1_ref, c2_ref, c3_ref)

    @pl.when((b == 0) & (j == 0))
    def _():
        o_ref[...] = jnp.zeros(o_ref.shape, F32)

    @pl.when(j == 0)
    def _():
        m_ref[...] = jnp.full(m_ref.shape, NEG, F32)
        l_ref[...] = jnp.zeros(l_ref.shape, F32)
        acc_ref[...] = jnp.zeros(acc_ref.shape, F32)

    row = proj_ref[pl.ds(b, 1), :]
    qh = _head_rows(row[:, 0:QW], NSA_HEADS).astype(BF16)

    def group_q(g):
        qg = jnp.concatenate([qh[g * rr:(g + 1) * rr], jnp.zeros((8 - rr, HEAD_DIM), BF16)], axis=0)
        slope = jnp.concatenate([jnp.full((1, 1), sl_ref[g, r], F32) for r in range(rr)]
                                + [jnp.zeros((8 - rr, 1), F32)], axis=0)
        return qg, slope

    for g in range(NSA_KV_HEADS):
        qg, slope = group_q(g)
        blk_id = idx_ref[(b * NSA_KV_HEADS + g) * NSA_TOPK + j]
        kk = c_refs[g][:, 0, g, :].astype(BF16)
        vv = c_refs[g][:, 1, g, :].astype(BF16)
        dist = (past - (blk_id * NSA_BLOCK + _iota((1, NSA_BLOCK), 1))).astype(F32)
        sc = _dot_nt(qg, kk) * scale - slope * dist
        rows = slice(g * 8, (g + 1) * 8)
        m_old = m_ref[rows, 0:1]
        m_new = jnp.maximum(m_old, jnp.max(sc, axis=-1, keepdims=True))
        alpha = jnp.exp(m_old - m_new)
        e = jnp.exp(sc - m_new)
        l_ref[rows, :] = alpha * l_ref[rows, :] + jnp.sum(e, axis=-1, keepdims=True)
        acc_ref[rows, :] = alpha * acc_ref[rows, :] + jnp.dot(e.astype(BF16), vv, preferred_element_type=F32)
        m_ref[rows, :] = jnp.broadcast_to(m_new, (8, HEAD_DIM))

    @pl.when(j == n_sel - 1)
    def _():
        gt = _sigmoid(gt_ref[pl.ds(b, 1), :])
        pieces = []
        for g in range(NSA_KV_HEADS):
            qg, slope = group_q(g)
            qf = qg.astype(F32)
            base = QW + g * HEAD_DIM
            k_sel = row[:, base + 2 * KVW: base + 2 * KVW + HEAD_DIM].astype(BF16).astype(F32)
            v_sel = row[:, base + 3 * KVW: base + 3 * KVW + HEAD_DIM].astype(BF16).astype(F32)
            k_win = row[:, base + 4 * KVW: base + 4 * KVW + HEAD_DIM].astype(BF16).astype(F32)
            v_win = row[:, base + 5 * KVW: base + 5 * KVW + HEAD_DIM].astype(BF16).astype(F32)
            rows = slice(g * 8, (g + 1) * 8)
            s_new = jnp.sum(qf * k_sel, axis=-1, keepdims=True) * scale
            m_old = m_ref[rows, 0:1]
            m_new = jnp.maximum(m_old, s_new)
            alpha = jnp.exp(m_old - m_new)
            e_new = jnp.exp(s_new - m_new)
            l = alpha * l_ref[rows, 0:1] + e_new
            o_s = (alpha * acc_ref[rows, :] + e_new.astype(BF16).astype(F32) * v_sel) / jnp.maximum(l, 1e-30)
            nwin = win_ref.shape[0]
            kw = win_ref[:, 0, g, :].astype(BF16)
            vw = win_ref[:, 1, g, :].astype(BF16)
            dist = nwin - _iota((1, nwin), 1)
            sw = _dot_nt(qg, kw) * scale - slope * dist.astype(F32)
            valid = dist < NSA_WINDOW
            sw = jnp.where(valid, sw, NEG)
            sw_new = jnp.sum(qf * k_win, axis=-1, keepdims=True) * scale
            mw = jnp.maximum(jnp.max(sw, axis=-1, keepdims=True), sw_new)
            ew = jnp.where(valid, jnp.exp(sw - mw), 0.0)
            ew_new = jnp.exp(sw_new - mw)
            den = jnp.maximum(jnp.sum(ew, axis=-1, keepdims=True) + ew_new, 1e-30)
            o_w = (jnp.dot((ew / den).astype(BF16), vw, preferred_element_type=F32)
                   + (ew_new / den).astype(BF16).astype(F32) * v_win)
            o_c = oc_ref[rows, :]
            for r in range(rr):
                c = g * 3 * rr + 3 * r
                pieces.append(gt[:, c:c + 1] * o_c[r:r + 1] + gt[:, c + 1:c + 2] * o_s[r:r + 1]
                              + gt[:, c + 2:c + 3] * o_w[r:r + 1])
        o_ref[pl.ds(b, 1), :] = jnp.concatenate(pieces, axis=1)


def _nsa_sel_sample(proj, small, cache, win_cache, e, page_table, idx, o_cmp, slopes, prev, row0, past):
    nd = page_table.shape[0]
    page_rows = cache.shape[2]
    per = page_rows // NSA_BLOCK
    kvh, hd = cache.shape[4], cache.shape[5]
    nwin = win_cache.shape[2]
    n_sel = NSA_TOPK - 1
    rb = row0 // SAMPLE_PAD

    def cache_spec(g):
        def imap(b, j, idx_r, pt_r):
            blk = idx_r[(b * NSA_KV_HEADS + g) * NSA_TOPK + j]
            return (e, pt_r[b, blk // per], blk % per, 1, 0, 0)
        return pl.BlockSpec((None, None, NSA_BLOCK, 2, kvh, hd), imap)

    grid_spec = pltpu.PrefetchScalarGridSpec(
        num_scalar_prefetch=2, grid=(nd, n_sel),
        in_specs=[pl.BlockSpec(memory_space=pltpu.SMEM),
                  pl.BlockSpec((SAMPLE_PAD, proj.shape[1]), lambda b, j, i_r, p_r: (rb, 0)),
                  cache_spec(0), cache_spec(1), cache_spec(2), cache_spec(3),
                  pl.BlockSpec((None, None, nwin, 2, kvh, hd), lambda b, j, i_r, p_r: (e, b, 0, 0, 0, 0)),
                  pl.BlockSpec((None, 8 * NSA_KV_HEADS, HEAD_DIM), lambda b, j, i_r, p_r: (b, 0, 0)),
                  pl.BlockSpec((SAMPLE_PAD, small.shape[1]), lambda b, j, i_r, p_r: (rb, 0)),
                  pl.BlockSpec(memory_space=pl.ANY)],
        out_specs=pl.BlockSpec((SAMPLE_PAD, QW), lambda b, j, i_r, p_r: (rb, 0)),
        scratch_shapes=[pltpu.VMEM((8 * NSA_KV_HEADS, HEAD_DIM), F32)] * 3)
    return pl.pallas_call(
        functools.partial(_nsa_sel_sample_kernel, past=past, n_sel=n_sel, nd=nd),
        grid_spec=grid_spec,
        out_shape=jax.ShapeDtypeStruct(prev.shape, F32),
        input_output_aliases={11: 0},
        compiler_params=_cparams(("arbitrary", "arbitrary")), name="nsa_sel_sample",
    )(idx, page_table, slopes, proj, cache, cache, cache, cache, win_cache, o_cmp, small, prev)


def _stack_heads(x, n):
    return jnp.stack([x[:, h * HEAD_DIM:(h + 1) * HEAD_DIM] for h in range(n)], axis=0)


def _l2n(x):
    return x * lax.rsqrt(jnp.sum(x * x, axis=-1, keepdims=True) + EPS)


def _softplus(x):
    return jnp.maximum(x, 0.0) + jnp.log(1.0 + jnp.exp(-jnp.abs(x)))


def _bmm(a, b, **kw):
    return jnp.einsum('hcs,hsd->hcd', a, b, preferred_element_type=F32, **kw)


def _bmm_nt(a, b, **kw):
    return jnp.einsum('hcd,hsd->hcs', a, b, preferred_element_type=F32, **kw)


def _bmm_tn(a, b, **kw):
    return jnp.einsum('hcd,hce->hde', a, b, preferred_element_type=F32, **kw)


def _split_bf16(x):
    hi = x.astype(BF16)
    return hi, (x - hi.astype(F32)).astype(BF16)


def _bmm_split(ah, al, bh, bl):
    return _bmm(ah, bh) + (_bmm(ah, bl) + _bmm(al, bh))


def _gdn_intra_kernel(x_ref, sm_ref, al_ref, dt_ref, qe_ref, ke_ref, u_ref, w_ref, qk_ref, egl_ref, *, a_col, b_col):
    nh, cc = GDN_HEADS, x_ref.shape[0]
    x = x_ref[...]
    q = _l2n(_stack_heads(x[:, 0:GH], nh)) * (HEAD_DIM ** -0.5)
    k = _l2n(_stack_heads(x[:, GH:2 * GH], nh))
    v = _stack_heads(x[:, 2 * GH:3 * GH], nh)
    sm = sm_ref[...]
    gdec = -jnp.exp(al_ref[...]) * _softplus(sm + dt_ref[...])
    ri, ci = _iota((cc, cc), 0), _iota((cc, cc), 1)
    tri = ri >= ci
    strict = ri > ci
    gcum = jnp.dot(tri.astype(F32), gdec, preferred_element_type=F32, precision=HIGHEST)
    beta = _sigmoid(sm)
    gc = jnp.stack([gcum[:, a_col + h:a_col + h + 1] for h in range(nh)], axis=0)
    bc = jnp.stack([beta[:, b_col + h:b_col + h + 1] for h in range(nh)], axis=0)
    gr = jnp.sum(jnp.where((ri == ci)[None], jnp.broadcast_to(gc, (nh, cc, cc)), 0.0), axis=1, keepdims=True)
    decay = jnp.where(tri[None], jnp.exp(jnp.where(tri[None], gc - gr, 0.0)), 0.0)
    kb = k * bc
    vb = v * bc
    a = jnp.where(strict[None], _bmm_nt(kb, k) * decay, 0.0)
    neg = -a
    tm = jnp.where((ri == ci)[None], 1.0, 0.0) + neg
    pw = neg
    span = 2
    while span < cc:
        ph, plo = _split_bf16(pw)
        pw = _bmm_split(ph, plo, ph, plo)
        ph, plo = _split_bf16(pw)
        th, tlo = _split_bf16(tm)
        tm = tm + _bmm_split(th, tlo, ph, plo)
        span *= 2
    eg = jnp.exp(gc)
    gl = gc[:, cc - 1:cc, :]
    u = _bmm(tm, vb)
    w = _bmm(tm, kb * eg)
    qk = jnp.where(tri[None], _bmm_nt(q, k) * decay, 0.0)
    qe = q * eg
    ke = k * jnp.exp(gl - gc)
    egl = jnp.exp(gl)
    for h in range(nh):
        cols = slice(h * HEAD_DIM, (h + 1) * HEAD_DIM)
        qe_ref[:, cols] = qe[h]
        ke_ref[:, cols] = ke[h]
        u_ref[:, cols] = u[h]
        w_ref[:, cols] = w[h]
        qk_ref[:, cols] = jnp.concatenate([qk[h], jnp.zeros((cc, HEAD_DIM - cc), F32)], axis=1)
        egl_ref[:, cols] = jnp.broadcast_to(egl[h], (8, HEAD_DIM))


def _gdn_intra(qkv_c, small, a_log_row, dt_row, n_rows, a_col, b_col):
    cc = GDN_CHUNK
    assert cc <= HEAD_DIM
    nch = n_rows // cc
    big = jax.ShapeDtypeStruct((n_rows, GH), F32)
    blk = pl.BlockSpec((cc, GH), lambda i: (i, 0))
    return pl.pallas_call(
        functools.partial(_gdn_intra_kernel, a_col=a_col, b_col=b_col),
        grid=(nch,),
        in_specs=[pl.BlockSpec((cc, 3 * GH), lambda i: (i, 0)),
                  pl.BlockSpec((cc, small.shape[1]), lambda i: (i, 0)),
                  pl.BlockSpec((1, small.shape[1]), lambda i: (0, 0)),
                  pl.BlockSpec((1, small.shape[1]), lambda i: (0, 0))],
        out_specs=[blk] * 5 + [pl.BlockSpec((None, 8, GH), lambda i: (i, 0, 0))],
        out_shape=[big] * 5 + [jax.ShapeDtypeStruct((nch, 8, GH), F32)],
        compiler_params=_cparams(("parallel",)), name="gdn_intra_chunk",
    )(qkv_c, small, a_log_row, dt_row)


def _gdn_scan_kernel(qe_ref, ke_ref, u_ref, w_ref, qk_ref, egl_ref, z_ref, g_ref, o_ref, s_ref):
    nh, cc = GDN_HEADS, qe_ref.shape[0]

    @pl.when(pl.program_id(1) == 0)
    def _():
        s_ref[...] = jnp.zeros(s_ref.shape, F32)

    s = s_ref[...]
    qe = _stack_heads(qe_ref[...], nh)
    ke = _stack_heads(ke_ref[...], nh)
    u = _stack_heads(u_ref[...], nh)
    w = _stack_heads(w_ref[...], nh)
    qk = _stack_heads(qk_ref[...], nh)[:, :, 0:cc]
    egl = _stack_heads(egl_ref[...], nh)[:, 0:1, :]
    v_new = u - _bmm(w, s)
    o = _bmm(qe, s) + _bmm(qk, v_new)
    s_ref[...] = s * egl + _bmm_tn(ke, v_new)
    z = _stack_heads(z_ref[...], nh)
    o = _rms(o) * g_ref[...][None] * (z * _sigmoid(z))
    for h in range(nh):
        o_ref[:, h * HEAD_DIM:(h + 1) * HEAD_DIM] = o[h]


def _gdn_scan(intra, gz, gdn_g, mp, batch, seq):
    cc = GDN_CHUNK
    nch = seq // cc
    qe, ke, u, w, qk, egl = intra
    blk = pl.BlockSpec((cc, GH), lambda b, n: (b * nch + n, 0))
    return pl.pallas_call(
        _gdn_scan_kernel,
        grid=(batch, nch),
        in_specs=[blk] * 5 + [pl.BlockSpec((None, 8, GH), lambda b, n: (b * nch + n, 0, 0)),
                              pl.BlockSpec((cc, GH), lambda b, n: (b * nch + n, 3)),
                              pl.BlockSpec((1, HEAD_DIM), lambda b, n: (0, 0))],
        out_specs=[blk, pl.BlockSpec((None, GDN_HEADS, HEAD_DIM, HEAD_DIM), lambda b, n: (b, 0, 0, 0))],
        out_shape=[jax.ShapeDtypeStruct((mp, GH), F32),
                   jax.ShapeDtypeStruct((batch, GDN_HEADS, HEAD_DIM, HEAD_DIM), F32)],
        compiler_params=_cparams(("parallel", "arbitrary")), name="gdn_scan",
    )(qe, ke, u, w, qk, egl, gz, gdn_g.reshape(1, HEAD_DIM))


def _col_of_row(row, n):
    eye = _iota((n, n), 0) == _iota((n, n), 1)
    return jnp.sum(jnp.where(eye, jnp.broadcast_to(row, (n, n)), 0.0), axis=1, keepdims=True)


def _gdn_sample_kernel(x_ref, h_ref, cw_ref, gz_ref, sm_ref, al_ref, dt_ref, g_ref, s0_ref, prev_ref,
                       o_ref, s_ref, *, a_col, b_col):
    del prev_ref
    b = pl.program_id(0)
    nh = GDN_HEADS

    @pl.when(b == 0)
    def _():
        o_ref[...] = jnp.zeros(o_ref.shape, F32)

    nk = cw_ref.shape[0]
    conv = jnp.sum(h_ref[...] * cw_ref[0:nk - 1, :], axis=0, keepdims=True) + x_ref[pl.ds(b, 1), :] * cw_ref[nk - 1:nk, :]
    conv = conv * _sigmoid(conv)
    q = _l2n(_head_rows(conv[:, 0:GH], nh)) * (HEAD_DIM ** -0.5)
    k = _l2n(_head_rows(conv[:, GH:2 * GH], nh))
    v = _head_rows(conv[:, 2 * GH:3 * GH], nh)
    sm = sm_ref[pl.ds(b, 1), :]
    gdec = -jnp.exp(al_ref[...]) * _softplus(sm + dt_ref[...])
    eg = jnp.exp(_col_of_row(gdec[:, a_col:a_col + nh], nh))
    beta = _col_of_row(_sigmoid(sm)[:, b_col:b_col + nh], nh)
    s = s0_ref[...]
    pad = lambda t: jnp.concatenate([t[:, None, :], jnp.zeros((nh, 7, HEAD_DIM), F32)], axis=1)
    ks = _bmm(pad(k), s, precision=HIGHEST)[:, 0, :]
    v_new = beta * (v - eg * ks)
    o = eg * _bmm(pad(q), s, precision=HIGHEST)[:, 0, :] + jnp.sum(q * k, axis=-1, keepdims=True) * v_new
    s_ref[...] = s * eg[:, :, None] + _bmm_tn(pad(k), pad(v_new), precision=HIGHEST)
    z = _head_rows(gz_ref[pl.ds(b, 1), :], nh)
    o = _rms(o) * g_ref[...] * (z * _sigmoid(z))
    o_ref[pl.ds(b, 1), :] = jnp.concatenate([o[h:h + 1] for h in range(nh)], axis=1)


def _gdn_sample(gz, hist, conv_w, small, a_log_row, dt_row, gdn_g, s0, prev, row0, a_col, b_col):
    nd = hist.shape[0]
    rb = row0 // SAMPLE_PAD
    full = lambda shp: pl.BlockSpec(shp, lambda b: (0,) * len(shp))
    return pl.pallas_call(
        functools.partial(_gdn_sample_kernel, a_col=a_col, b_col=b_col),
        grid=(nd,),
        in_specs=[pl.BlockSpec((SAMPLE_PAD, 3 * GH), lambda b: (rb, 0)),
                  pl.BlockSpec((None, GDN_CONV - 1, 3 * GH), lambda b: (b, 0, 0)),
                  full(conv_w.shape),
                  pl.BlockSpec((SAMPLE_PAD, GH), lambda b: (rb, 3)),
                  pl.BlockSpec((SAMPLE_PAD, small.shape[1]), lambda b: (rb, 0)),
                  full(a_log_row.shape), full(dt_row.shape), full((1, HEAD_DIM)),
                  pl.BlockSpec((None, GDN_HEADS, HEAD_DIM, HEAD_DIM), lambda b: (b, 0, 0, 0)),
                  pl.BlockSpec(memory_space=pl.ANY)],
        out_specs=[pl.BlockSpec((SAMPLE_PAD, GH), lambda b: (rb, 0)),
                   pl.BlockSpec((None, GDN_HEADS, HEAD_DIM, HEAD_DIM), lambda b: (b, 0, 0, 0))],
        out_shape=[jax.ShapeDtypeStruct(prev.shape, F32), jax.ShapeDtypeStruct(s0.shape, F32)],
        input_output_aliases={9: 0},
        compiler_params=_cparams(("arbitrary",)), name="gdn_sample",
    )(gz, hist, conv_w, gz, small, a_log_row, dt_row, gdn_g.reshape(1, HEAD_DIM), s0, prev)


def _even_layer(hn, e, dims, slopes, cache_nsa_kv, cache_nsa_win, state_gdn, state_gdn_conv, page_table,
                w_in, cmp_w, conv_w, a_log, dt_bias, gdn_g, w_out):
    batch, seq, nd, d = dims
    rows_p = batch * seq
    mp = hn.shape[0]
    past = page_table.shape[1] * cache_nsa_kv.shape[2]
    c_nsa = QW + 6 * KVW
    c_gate = c_nsa
    c_gdn = c_gate + 3 * NSA_HEADS
    c_a = c_gdn + 4 * GH
    a_col, b_col = 3 * NSA_HEADS, 3 * NSA_HEADS + GDN_HEADS
    hd = HEAD_DIM

    proj = _matmul([hn], [(w_in, e, d, 0, 0)], (0,), _ep_plain, c_nsa, F32, _col_tile(c_nsa, 512), "proj_nsa")
    cols = lambda c0, c1: lax.slice(w_in, (e, 0, c0), (e + 1, d, c1))
    gz = _matmul([hn], [(cols(c_gdn, c_a), 0, d, 0, 0)], (0,), _ep_plain, 4 * GH, F32, _col_tile(4 * GH, 512), "proj_gdn")
    w_small = jnp.concatenate([cols(c_gate, c_gdn), cols(c_a, w_in.shape[2]),
                               jnp.zeros((1, d, hd - 3 * NSA_HEADS - 2 * GDN_HEADS), F32)], axis=2)
    small = _matmul([hn], [(w_small, 0, d, 0, 0)], (0,), _ep_plain, hd, F32, hd, "proj_small")

    w_exp = jnp.concatenate([jnp.repeat(cmp_w[:, :, 0], hd, axis=1), jnp.repeat(cmp_w[:, :, 1], hd, axis=1)], axis=1)
    kcvc = _pool_prompt(proj, w_exp, batch, seq)
    gates_t = small[:, 0:3 * NSA_HEADS].reshape(mp, NSA_KV_HEADS, 3 * NSA_GROUP).transpose(1, 0, 2)
    o_nsa = _nsa_prompt(proj, kcvc, gates_t, slopes, batch, seq)
    wk = jnp.broadcast_to(cmp_w[:, :, 0][:, :, None], (NSA_BLOCK, NSA_KV_HEADS, hd))
    wv = jnp.broadcast_to(cmp_w[:, :, 1][:, :, None], (NSA_BLOCK, NSA_KV_HEADS, hd))
    kcs, vcs = _pool_sample(cache_nsa_kv, e, page_table, wk, wv)
    o_cmp, sel_rows = _nsa_cmp_sample(proj, kcs, vcs, slopes, rows_p, past)
    idx = sel_rows[:, :, :, 0].reshape(-1)
    o_nsa = _nsa_sel_sample(proj, small, cache_nsa_kv, cache_nsa_win, e, page_table, idx, o_cmp, slopes, o_nsa,
                            rows_p, past)

    pad_row = lambda vec, col: jnp.zeros((1, hd), F32).at[0, col:col + GDN_HEADS].set(vec.astype(F32))
    a_log_row, dt_row = pad_row(a_log, a_col), pad_row(dt_bias, a_col)
    qkv_c = _dwconv_prompt(gz, conv_w, jnp.zeros((3 * GH,), F32), 0, 3 * GH, batch, seq, True, "gdn_conv_prompt")
    intra = _gdn_intra(qkv_c, small, a_log_row, dt_row, rows_p, a_col, b_col)
    o_gdn, s_prompt = _gdn_scan(intra, gz, gdn_g, mp, batch, seq)
    o_gdn, s_sample = _gdn_sample(gz, state_gdn_conv[e], conv_w, small, a_log_row, dt_row, gdn_g, state_gdn[e], o_gdn,
                                  rows_p, a_col, b_col)

    mixed = _matmul([o_nsa.astype(BF16), o_gdn.astype(BF16)], [(w_out, e, QW, 0, 0), (w_out, e, GH, 1, 0)], (0, 1),
                    _ep_sum, d, F32, _col_tile(d, 512), "proj_out")

    kv_rows = proj[:, QW:QW + 4 * KVW]
    win_rows = proj[:, QW + 4 * KVW:QW + 6 * KVW]
    kv_p = kv_rows[:rows_p].reshape(batch, seq, 4, NSA_KV_HEADS, hd)
    kv_s = kv_rows[rows_p:rows_p + nd].reshape(nd, 1, 4, NSA_KV_HEADS, hd)
    nwin_p = min(NSA_WINDOW, seq)
    win_p = win_rows[:rows_p].reshape(batch, seq, 2, NSA_KV_HEADS, hd)[:, seq - nwin_p:]
    win_all = jnp.concatenate([cache_nsa_win[e], win_rows[rows_p:rows_p + nd].reshape(nd, 1, 2, NSA_KV_HEADS, hd)], axis=1)
    win_s = win_all[:, win_all.shape[1] - min(NSA_WINDOW, win_all.shape[1]):]
    qkv_g = gz[:, 0:3 * GH]
    gconv_p = qkv_g[:rows_p].reshape(batch, seq, 3 * GH)[:, seq - (GDN_CONV - 1):]
    gconv_s = jnp.concatenate([state_gdn_conv[e], qkv_g[rows_p:rows_p + nd][:, None]], axis=1)[:, 1:]
    return mixed, (kv_p, kv_s, win_p, win_s, s_prompt, s_sample, gconv_p, gconv_s)


def _odd_layer(hn, o, dims, state_conf_conv, w_pw1, w_dw, b_dw, ln_g, ln_b, w_pw2):
    batch, seq, nd, d = dims
    rows_p = batch * seq
    ch = w_dw.shape[1]
    tn = _col_tile(ch, 256)
    u = _matmul([hn], [(w_pw1, o, d, 0, 0), (w_pw1, o, d, 0, ch // tn)], (0, 0), _ep_glu, ch, F32, tn, "conf_pw1_glu")
    c = _dwconv_prompt(u, w_dw, b_dw, 0, ch, batch, seq, False, "conf_conv_prompt")
    c = _dwconv_sample(u, state_conf_conv[o], w_dw, b_dw, 0, ch, c, rows_p, False, "conf_conv_sample")
    y = _ln_silu(c, ln_g, ln_b)
    mixed = _matmul([y], [(w_pw2, o, ch, 0, 0)], (0,), _ep_plain, d, F32, _col_tile(d, 512), "conf_pw2")
    cc_p = u[:rows_p].reshape(batch, seq, ch)[:, seq - (CONF_WIDTH - 1):]
    cc_s = jnp.concatenate([state_conf_conv[o], u[rows_p:rows_p + nd][:, None]], axis=1)[:, 1:]
    return mixed, (cc_p, cc_s)


def kernel(x_prompt, x_sample, cache_nsa_kv, cache_nsa_win, state_gdn, state_gdn_conv, state_conf_conv, page_table,
           p_prompt, p_sample, norm_g, w_in_even, cmp_pool_w, gdn_conv_w, gdn_a_log, gdn_dt_bias, gdn_norm_g,
           w_out_even, w_pw1, w_dw, b_dw, conf_ln_g, conf_ln_b, w_pw2, w_ffn_gate, w_ffn_up, w_ffn_down,
           w_ple_gate, w_ple_proj):
    batch, seq, d = x_prompt.shape
    nd = x_sample.shape[0]
    depth = norm_g.shape[0]
    ffn = w_ffn_gate.shape[2]
    ple = w_ple_proj.shape[1]
    rows_p = batch * seq
    assert x_sample.shape[1] == 1 and nd <= 8 and rows_p % SAMPLE_PAD == 0
    dims = (batch, seq, nd, d)
    slopes = jnp.asarray(np.asarray([2.0 ** (-8.0 * (h + 1) / NSA_HEADS) for h in range(NSA_HEADS)], np.float32)
                         .reshape(NSA_KV_HEADS, NSA_GROUP))

    padrows = lambda a: jnp.concatenate([a, jnp.zeros((SAMPLE_PAD - nd,) + a.shape[1:], a.dtype)], axis=0)
    h = jnp.concatenate([x_prompt.reshape(rows_p, d), padrows(x_sample.reshape(nd, d))], axis=0)
    ones = jnp.ones((d,), F32)
    tn_f = _col_tile(ffn, 256)
    w_down_bf16 = w_ffn_down.astype(BF16)
    outs = [[] for _ in range(10)]
    hn = _norm(h, norm_g[0, 0])
    for i in range(depth):
        g_post, f_pre, f_post = norm_g[i, 1], norm_g[i, 2], norm_g[i, 3]
        if i % 2 == 0:
            e = i // 2
            mixed, st = _even_layer(hn, e, dims, slopes, cache_nsa_kv, cache_nsa_win, state_gdn, state_gdn_conv,
                                    page_table, w_in_even, cmp_pool_w[e], gdn_conv_w[e], gdn_a_log[e],
                                    gdn_dt_bias[e], gdn_norm_g[e], w_out_even)
            for lst, val in zip(outs[:8], st):
                lst.append(val)
        else:
            o = i // 2
            mixed, st = _odd_layer(hn, o, dims, state_conf_conv, w_pw1, w_dw[o], b_dw[o], conf_ln_g[o],
                                   conf_ln_b[o], w_pw2)
            outs[8].append(st[0])
            outs[9].append(st[1])
        h, hn = _addnorm(h, mixed, g_post, f_pre)
        act = _matmul([hn], [(w_ffn_gate, i, d, 0, 0), (w_ffn_up, i, d, 0, 0)], (0, 0), _ep_swiglu, ffn, BF16, tn_f,
                      "ffn_gate_up")
        f = _matmul([act], [(w_down_bf16, i, ffn, 0, 0)], (0,), _ep_plain, d, F32, _col_tile(d, 512), "ffn_down",
                    x_stationary=True)
        h, hn = _addnorm(h, f, f_post, ones)
        p_i = jnp.concatenate([p_prompt[i].reshape(rows_p, ple), padrows(p_sample[i].reshape(nd, ple))], axis=0).astype(BF16)
        h = _matmul([hn, p_i], [(w_ple_gate, i, d, 0, 0), (w_ple_proj, i, ple, 0, 0)], (0, 1), _ep_ple, d, F32,
                    _col_tile(d, 512), "ple", res=h)
        if i + 1 < depth:
            hn = _norm(h, norm_g[i + 1, 0])
    y_p = h[:rows_p].reshape(batch, seq, d)
    y_s = h[rows_p:rows_p + nd].reshape(nd, 1, d)
    return (y_p, y_s) + tuple(jnp.stack(lst) for lst in outs)
```

```python
import functools
import math

import numpy as np
import jax
import jax.numpy as jnp
from jax import lax
from jax.experimental import pallas as pl
from jax.experimental.pallas import tpu as pltpu

F32 = jnp.float32
BF16 = jnp.bfloat16
I32 = jnp.int32
HIGHEST = lax.Precision.HIGHEST

HEAD_DIM = 128
NSA_HEADS = 16
NSA_KV_HEADS = 4
NSA_GROUP = NSA_HEADS // NSA_KV_HEADS
NSA_BLOCK = 64
NSA_TOPK = 16
NSA_WINDOW = 512
NSA_FORCE = 1.0e4
GDN_HEADS = 16
GDN_CONV = 4
GDN_CHUNK = 64
CONF_WIDTH = 31
EPS = 1e-6
NEG = -1.0e30

SAMPLE_PAD = 16
V7X_VMEM_LIMIT = 58 * 1024 * 1024
MM_VMEM_BUDGET = 50 * 1024 * 1024
MM_MAX_ROWS = 1024
NSA_QTILE = 256
SEL_KCHUNK = 512
POOL_PAGES_PER_STEP = 8

QW = NSA_HEADS * HEAD_DIM
GW = NSA_GROUP * HEAD_DIM
KVW = NSA_KV_HEADS * HEAD_DIM
GH = GDN_HEADS * HEAD_DIM


def _cparams(sem):
    return pltpu.CompilerParams(dimension_semantics=sem, vmem_limit_bytes=V7X_VMEM_LIMIT)


def _divisor_tile(n, target, mult):
    best = None
    for t in range(mult, min(n, target) + 1, mult):
        if n % t == 0:
            best = t
    assert best is not None, (n, target, mult)
    return best


def _sigmoid(x):
    return 1.0 / (1.0 + jnp.exp(-x))


def _iota(shape, dim):
    return lax.broadcasted_iota(I32, shape, dim)


def _rms(x):
    return x * lax.rsqrt(jnp.mean(x * x, axis=-1, keepdims=True) + EPS)


def _msoftmax(s, mask):
    s = jnp.where(mask, s, NEG)
    m = jnp.max(s, axis=-1, keepdims=True)
    e = jnp.where(mask, jnp.exp(s - m), 0.0)
    return e / jnp.maximum(jnp.sum(e, axis=-1, keepdims=True), 1e-30)


def _dot_nt(a, b, **kw):
    return lax.dot_general(a, b, (((1,), (1,)), ((), ())), preferred_element_type=F32, **kw)


def _norm_kernel(h_ref, g_ref, o_ref):
    o_ref[...] = (_rms(h_ref[...]) * g_ref[...]).astype(o_ref.dtype)


def _addnorm_kernel(h_ref, f_ref, g1_ref, g2_ref, h2_ref, hn_ref):
    h2 = h_ref[...] + _rms(f_ref[...]) * g1_ref[...]
    h2_ref[...] = h2
    hn_ref[...] = (_rms(h2) * g2_ref[...]).astype(hn_ref.dtype)


def _ln_silu_kernel(c_ref, g_ref, b_ref, o_ref):
    x = c_ref[...]
    xc = x - jnp.mean(x, axis=-1, keepdims=True)
    y = xc * lax.rsqrt(jnp.mean(xc * xc, axis=-1, keepdims=True) + EPS)
    y = y * g_ref[...] + b_ref[...]
    o_ref[...] = (y * _sigmoid(y)).astype(o_ref.dtype)


def _row_call(kernel, mats, vecs, out_dtypes, name):
    mp, d = mats[0].shape
    tr = _divisor_tile(mp, 160, 16)
    mat_spec = pl.BlockSpec((tr, d), lambda i: (i, 0))
    vec_spec = pl.BlockSpec((1, d), lambda i: (0, 0))
    outs = [jax.ShapeDtypeStruct((mp, d), dt) for dt in out_dtypes]
    res = pl.pallas_call(
        kernel, grid=(mp // tr,),
        in_specs=[mat_spec] * len(mats) + [vec_spec] * len(vecs),
        out_specs=[mat_spec] * len(outs), out_shape=outs,
        compiler_params=_cparams(("parallel",)), name=name,
    )(*mats, *[v.reshape(1, d).astype(F32) for v in vecs])
    return res


def _norm(h, g):
    return _row_call(_norm_kernel, [h], [g], [BF16], "rmsnorm")[0]


def _addnorm(h, f, g1, g2):
    return _row_call(_addnorm_kernel, [h, f], [g1, g2], [F32, BF16], "add_rmsnorm")


def _ln_silu(c, g, b):
    return _row_call(_ln_silu_kernel, [c], [g, b], [BF16], "layernorm_silu")[0]


def _ep_plain(accs, res):
    return accs[0]


def _ep_sum(accs, res):
    return accs[0] + accs[1]


def _ep_swiglu(accs, res):
    return accs[0] * _sigmoid(accs[0]) * accs[1]


def _ep_glu(accs, res):
    return accs[0] * _sigmoid(accs[1])


def _ep_ple(accs, res):
    return res + _sigmoid(accs[0]) * accs[1]


def _w_pieces(wspec, tn):
    return wspec[5] if len(wspec) == 6 else [(tn, wspec[4], 1, 0, tn)]


def _mm_kernel(*refs, x_of_w, pieces, epilogue, has_res, cast):
    nx = max(x_of_w) + 1
    n_piece = [len(p) for p in pieces]
    x_refs = refs[:nx]
    pos = nx
    w_refs = []
    for n in n_piece:
        w_refs.append(refs[pos:pos + n])
        pos += n
    res_ref = refs[pos] if has_res else None
    pos += int(has_res)
    o_ref = refs[pos]
    wb_refs = refs[pos + 1:] if cast else [r[0] for r in w_refs]

    if cast:
        @pl.when(pl.program_id(1) == 0)
        def _():
            for prefs, pcs, wb_ref in zip(w_refs, pieces, wb_refs):
                parts = [r[...][:, l0:l1] if (l0, l1) != (0, bw) else r[...] for r, (bw, _, _, l0, l1) in zip(prefs, pcs)]
                fill = wb_ref.shape[1] - sum(l1 - l0 for (_, _, _, l0, l1) in pcs)
                if fill:
                    parts.append(jnp.zeros((wb_ref.shape[0], fill), F32))
                tile = parts[0] if len(parts) == 1 else jnp.concatenate(parts, axis=1)
                wb_ref[...] = tile.astype(BF16)

    accs = [jnp.dot(x_refs[xi][...], wb[...], preferred_element_type=F32) for xi, wb in zip(x_of_w, wb_refs)]
    res = res_ref[...] if has_res else None
    o_ref[...] = epilogue(accs, res).astype(o_ref.dtype)


def _mm_row_tile(mp, xs, ws, tn, out_dtype, has_res, x_stationary):
    fixed = 0
    for wspec in ws:
        w, k = wspec[0], wspec[2]
        fixed += 2 * k * sum(p[0] for p in _w_pieces(wspec, tn)) * w.dtype.itemsize
        fixed += k * tn * 2 if w.dtype != BF16 else 0
    per_row = sum(x.shape[1] * x.dtype.itemsize * (1 if x_stationary else 2) for x in xs)
    per_row += 2 * tn * jnp.dtype(out_dtype).itemsize + (2 * tn * 4 if has_res else 0)
    per_row += len(ws) * tn * 4
    best = SAMPLE_PAD
    for tm in range(SAMPLE_PAD, MM_MAX_ROWS + 1, SAMPLE_PAD):
        if mp % tm == 0 and fixed + tm * per_row <= MM_VMEM_BUDGET:
            best = tm
    return best


def _matmul(xs, ws, x_of_w, epilogue, n_out, out_dtype, tn, name, res=None, x_stationary=False):
    mp = xs[0].shape[0]
    assert n_out % tn == 0
    cast = not x_stationary
    assert all((wspec[0].dtype == BF16) == x_stationary for wspec in ws)
    tm = _mm_row_tile(mp, xs, ws, tn, out_dtype, res is not None, x_stationary)
    if x_stationary:
        grid = (mp // tm, n_out // tn)
        ij = lambda a, b: (a, b)
    else:
        grid = (n_out // tn, mp // tm)
        ij = lambda a, b: (b, a)
    x_mode = dict(pipeline_mode=pl.Buffered(1)) if x_stationary else {}
    in_specs = [pl.BlockSpec((tm, x.shape[1]), lambda a, b: (ij(a, b)[0], 0), **x_mode) for x in xs]
    arrs = list(xs)
    scratch = []
    pieces = [_w_pieces(wspec, tn) for wspec in ws]
    for wspec, pcs in zip(ws, pieces):
        w, layer, k, rb = wspec[:4]
        assert len(pcs) == 1 or cast
        for (bw, c0, step, _, _) in pcs:
            in_specs.append(pl.BlockSpec((None, k, bw), functools.partial(
                lambda a, b, layer, rb, c0, step: (layer, rb, c0 + step * ij(a, b)[1]), layer=layer, rb=rb, c0=c0, step=step)))
            arrs.append(w)
        if cast:
            scratch.append(pltpu.VMEM((k, tn), BF16))
    if res is not None:
        in_specs.append(pl.BlockSpec((tm, tn), lambda a, b: ij(a, b)))
        arrs.append(res)
    return pl.pallas_call(
        functools.partial(_mm_kernel, x_of_w=tuple(x_of_w), pieces=pieces, epilogue=epilogue, has_res=res is not None,
                          cast=cast),
        grid=grid,
        in_specs=in_specs,
        out_specs=pl.BlockSpec((tm, tn), lambda a, b: ij(a, b)),
        out_shape=jax.ShapeDtypeStruct((mp, n_out), out_dtype),
        scratch_shapes=scratch,
        compiler_params=_cparams(("arbitrary", "arbitrary")), name=name,
    )(*arrs)


def _col_tile(n, target):
    return _divisor_tile(n, target, 128)


def _dwconv_prompt_kernel(x_ref, w_ref, b_ref, o_ref, xs_ref, *, width, seq, pad, rows, act, batch):
    ct = x_ref.shape[1]

    @pl.when(pl.program_id(0) == batch)
    def _():
        o_ref[...] = jnp.zeros(o_ref.shape, F32)

    @pl.when(pl.program_id(0) < batch)
    def _():
        _dwconv_prompt_body(x_ref, w_ref, b_ref, o_ref, xs_ref, width=width, seq=seq, pad=pad, rows=rows, act=act)


def _dwconv_prompt_body(x_ref, w_ref, b_ref, o_ref, xs_ref, *, width, seq, pad, rows, act):
    ct = x_ref.shape[1]
    xs_ref[0:pad, :] = jnp.zeros((pad, ct), F32)
    xs_ref[pad:pad + seq, :] = x_ref[...]
    bias = b_ref[...]

    def body(i, carry):
        r0 = pl.multiple_of(i * rows, rows)
        acc = jnp.zeros((rows, ct), F32) + bias
        win = xs_ref[pl.ds(r0, rows + pad), :]
        off = pad - (width - 1)
        for r in range(8):
            taps = [j for j in range(width) if (off + j) % 8 == r]
            if not taps:
                continue
            wr = win if r == 0 else pltpu.roll(win, rows + pad - r, 0)
            for j in taps:
                base = off + j - r
                acc = acc + w_ref[pl.ds(j, 1), :] * wr[base:base + rows]
        if act:
            acc = acc * _sigmoid(acc)
        o_ref[pl.ds(r0, rows), :] = acc
        return carry

    lax.fori_loop(0, seq // rows, body, 0)


def _dwconv_prompt(x, w, bias, col0, ncols, batch, seq, act, name):
    mp = x.shape[0]
    width = w.shape[0]
    ct = _col_tile(ncols, 512)
    assert col0 % ct == 0
    pad = -(-(width - 1) // 8) * 8
    rows = 32
    return pl.pallas_call(
        functools.partial(_dwconv_prompt_kernel, width=width, seq=seq, pad=pad, rows=rows, act=act, batch=batch),
        grid=(batch + 1, ncols // ct),
        in_specs=[pl.BlockSpec((seq, ct), lambda b, j: (jnp.minimum(b, batch - 1), col0 // ct + j)),
                  pl.BlockSpec((width, ct), lambda b, j: (0, j)),
                  pl.BlockSpec((1, ct), lambda b, j: (0, j))],
        out_specs=pl.BlockSpec((seq, ct), lambda b, j: (b, j)),
        out_shape=jax.ShapeDtypeStruct((mp, ncols), F32),
        scratch_shapes=[pltpu.VMEM((pad + seq, ct), F32)],
        compiler_params=_cparams(("parallel", "parallel")), name=name,
    )(x, w, bias.reshape(1, ncols))


def _dwconv_sample_kernel(x_ref, h_ref, w_ref, b_ref, prev_ref, o_ref, *, width, nb, act):
    del prev_ref
    ct = x_ref.shape[1]
    acc = jnp.sum(h_ref[...] * w_ref[0:width - 1, :][None], axis=1)
    acc = acc + x_ref[0:nb, :] * w_ref[width - 1:width, :] + b_ref[...]
    if act:
        acc = acc * _sigmoid(acc)
    o_ref[0:nb, :] = acc
    o_ref[nb:, :] = jnp.zeros((SAMPLE_PAD - nb, ct), F32)


def _dwconv_sample(x, hist, w, bias, col0, ncols, prev, row0, act, name):
    nb = hist.shape[0]
    width = w.shape[0]
    ct = _col_tile(ncols, 512)
    rb = row0 // SAMPLE_PAD
    return pl.pallas_call(
        functools.partial(_dwconv_sample_kernel, width=width, nb=nb, act=act),
        grid=(ncols // ct,),
        in_specs=[pl.BlockSpec((SAMPLE_PAD, ct), lambda j: (rb, col0 // ct + j)),
                  pl.BlockSpec((nb, width - 1, ct), lambda j: (0, 0, j)),
                  pl.BlockSpec((width, ct), lambda j: (0, j)),
                  pl.BlockSpec((1, ct), lambda j: (0, j)),
                  pl.BlockSpec(memory_space=pl.ANY)],
        out_specs=pl.BlockSpec((SAMPLE_PAD, ct), lambda j: (rb, j)),
        out_shape=jax.ShapeDtypeStruct(prev.shape, F32),
        input_output_aliases={4: 0},
        compiler_params=_cparams(("parallel",)), name=name,
    )(x, hist, w, bias.reshape(1, ncols), prev)


def _pool_prompt_kernel(x_ref, w_ref, o_ref, *, nblk):
    x = x_ref[...].reshape(nblk, NSA_BLOCK, x_ref.shape[1])
    o_ref[...] = jnp.sum(x * w_ref[...][None], axis=1)


def _pool_prompt(proj, w_exp, batch, seq):
    nblk = seq // NSA_BLOCK
    return pl.pallas_call(
        functools.partial(_pool_prompt_kernel, nblk=nblk),
        grid=(batch,),
        in_specs=[pl.BlockSpec((seq, 2 * KVW), lambda b: (b, QW // (2 * KVW))),
                  pl.BlockSpec((NSA_BLOCK, 2 * KVW), lambda b: (0, 0))],
        out_specs=pl.BlockSpec((None, nblk, 2 * KVW), lambda b: (b, 0, 0)),
        out_shape=jax.ShapeDtypeStruct((batch, nblk, 2 * KVW), F32),
        compiler_params=_cparams(("parallel",)), name="nsa_pool_prompt",
    )(proj, w_exp)


def _nsa_prompt_kernel(sl_ref, q_ref, kc_ref, vc_ref, ks_ref, vs_ref, kw_ref, vw_ref, gt_ref, o_ref, *, tq, seq, batch):
    @pl.when(pl.program_id(0) == batch)
    def _():
        o_ref[...] = jnp.zeros(o_ref.shape, o_ref.dtype)

    @pl.when(pl.program_id(0) < batch)
    def _():
        _nsa_prompt_body(sl_ref, q_ref, kc_ref, vc_ref, ks_ref, vs_ref, kw_ref, vw_ref, gt_ref, o_ref, tq=tq, seq=seq)


def _nsa_prompt_body(sl_ref, q_ref, kc_ref, vc_ref, ks_ref, vs_ref, kw_ref, vw_ref, gt_ref, o_ref, *, tq, seq):
    g = pl.program_id(1)
    p0 = pl.program_id(2) * tq
    nblk = seq // NSA_BLOCK
    n_keep = min(NSA_TOPK, nblk)
    scale = HEAD_DIM ** -0.5
    rr = NSA_GROUP
    slopes = [sl_ref[g, r] for r in range(rr)]
    heads = [slice(r * tq, (r + 1) * tq) for r in range(rr)]
    tn_dims = (((0,), (0,)), ((), ()))

    q = q_ref[...]
    qs = jnp.concatenate([q[:, r * HEAD_DIM:(r + 1) * HEAD_DIM] for r in range(rr)], axis=0).astype(BF16)
    pos = p0 + _iota((tq, 1), 0)

    kc = kc_ref[...].astype(BF16)
    vc = vc_ref[...].astype(BF16)
    pos_l = p0 + _iota((1, tq), 1)
    blk = _iota((nblk, 1), 0)
    blk_end = blk * NSA_BLOCK + (NSA_BLOCK - 1)
    valid_c = blk_end <= pos_l
    dist_c = (pos_l - blk_end).astype(F32)
    st = _dot_nt(kc, qs) * scale
    probs = []
    for r in range(rr):
        sr = jnp.where(valid_c, st[:, heads[r]] - slopes[r] * dist_c, NEG)
        e = jnp.where(valid_c, jnp.exp(sr - jnp.max(sr, axis=0, keepdims=True)), 0.0)
        probs.append(e / jnp.maximum(jnp.sum(e, axis=0, keepdims=True), 1e-30))
    pt = jnp.concatenate(probs, axis=1)
    o_c = lax.dot_general(pt.astype(BF16), vc, tn_dims, preferred_element_type=F32)

    imp = probs[0]
    for r in range(1, rr):
        imp = imp + probs[r]
    cur = pos_l // NSA_BLOCK
    forced = (blk == 0) | (blk == cur) | (blk == cur - 1)
    imp = jnp.where(forced, NSA_FORCE, imp)
    imp = jnp.where(blk * NSA_BLOCK > pos_l, -1.0, imp)
    rank = jnp.zeros((nblk, tq), I32)
    for i in range(nblk):
        c = imp[i:i + 1, :]
        rank = rank + ((c > imp) | ((c == imp) & (blk > i))).astype(I32)
    sel_t = (rank < n_keep).astype(BF16)

    kch = min(SEL_KCHUNK, seq)
    n_chunks = (p0 + tq + kch - 1) // kch

    def sel_body(c, carry):
        m, l, acc = carry
        k0 = pl.multiple_of(c * kch, kch)
        kk = ks_ref[pl.ds(k0, kch), :].astype(BF16)
        vv = vs_ref[pl.ds(k0, kch), :].astype(BF16)
        kpos = k0 + _iota((1, kch), 1)
        expand = (_iota((nblk, kch), 0) == ((k0 + _iota((nblk, kch), 1)) // NSA_BLOCK)).astype(BF16)
        selk = lax.dot_general(sel_t, expand, tn_dims, preferred_element_type=F32) > 0.5
        mask = selk & (kpos <= pos)
        kb = (kpos - p0).astype(F32)
        qk = _dot_nt(qs, kk) * scale
        sc = jnp.concatenate([jnp.where(mask, qk[heads[r]] + slopes[r] * kb, NEG) for r in range(rr)], axis=0)
        m_new = jnp.maximum(m, jnp.max(sc, axis=-1, keepdims=True))
        alpha = jnp.exp(m - m_new)
        e = jnp.exp(sc - m_new)
        l = alpha * l + jnp.sum(e, axis=-1, keepdims=True)
        acc = alpha * acc + jnp.dot(e.astype(BF16), vv, preferred_element_type=F32)
        return m_new, l, acc

    init = (jnp.full((rr * tq, 1), NEG, F32), jnp.zeros((rr * tq, 1), F32), jnp.zeros((rr * tq, HEAD_DIM), F32))
    _, l, acc = lax.fori_loop(0, n_chunks, sel_body, init)
    o_s = acc / jnp.maximum(l, 1e-30)

    span = min(NSA_WINDOW + tq, seq)
    w0 = pl.multiple_of(jnp.clip(p0 + tq - span, 0, seq - span), 128)
    kk = kw_ref[pl.ds(w0, span), :].astype(BF16)
    vv = vw_ref[pl.ds(w0, span), :].astype(BF16)
    kpos = w0 + _iota((1, span), 1)
    dist = pos - kpos
    valid_w = (dist >= 0) & (dist < NSA_WINDOW)
    kb = (kpos - p0).astype(F32)
    qk = _dot_nt(qs, kk) * scale
    ews, lws = [], []
    for r in range(rr):
        sr = jnp.where(valid_w, qk[heads[r]] + slopes[r] * kb, NEG)
        e = jnp.exp(sr - jnp.max(sr, axis=-1, keepdims=True))
        ews.append(e)
        lws.append(jnp.sum(e, axis=-1, keepdims=True))
    o_w = jnp.dot(jnp.concatenate(ews, axis=0).astype(BF16), vv, preferred_element_type=F32)

    gt = _sigmoid(gt_ref[...])
    for r in range(rr):
        o_ref[:, r * HEAD_DIM:(r + 1) * HEAD_DIM] = (gt[:, 3 * r:3 * r + 1] * o_c[heads[r]]
                                                    + gt[:, 3 * r + 1:3 * r + 2] * o_s[heads[r]]
                                                    + gt[:, 3 * r + 2:3 * r + 3] * (o_w[heads[r]] / lws[r])
                                                    ).astype(o_ref.dtype)


def _nsa_prompt(proj, kcvc, gates_t, slopes, batch, seq):
    mp = proj.shape[0]
    tq = min(NSA_QTILE, seq)
    nq = seq // tq
    nblk = seq // NSA_BLOCK
    hb = HEAD_DIM
    c_sel = (QW + 2 * KVW) // hb
    c_win = (QW + 4 * KVW) // hb
    g4 = NSA_KV_HEADS
    bb = lambda b: jnp.minimum(b, batch - 1)
    row = lambda b, i: jnp.where(b < batch, b * nq + i, batch * nq)
    return pl.pallas_call(
        functools.partial(_nsa_prompt_kernel, tq=tq, seq=seq, batch=batch),
        grid=(batch + 1, NSA_KV_HEADS, nq),
        in_specs=[pl.BlockSpec(memory_space=pltpu.SMEM),
                  pl.BlockSpec((tq, GW), lambda b, g, i: (bb(b) * nq + i, g)),
                  pl.BlockSpec((None, nblk, hb), lambda b, g, i: (bb(b), 0, g)),
                  pl.BlockSpec((None, nblk, hb), lambda b, g, i: (bb(b), 0, g4 + g)),
                  pl.BlockSpec((seq, hb), lambda b, g, i: (bb(b), c_sel + g)),
                  pl.BlockSpec((seq, hb), lambda b, g, i: (bb(b), c_sel + g4 + g)),
                  pl.BlockSpec((seq, hb), lambda b, g, i: (bb(b), c_win + g)),
                  pl.BlockSpec((seq, hb), lambda b, g, i: (bb(b), c_win + g4 + g)),
                  pl.BlockSpec((None, tq, 3 * NSA_GROUP), lambda b, g, i: (g, bb(b) * nq + i, 0))],
        out_specs=pl.BlockSpec((tq, GW), lambda b, g, i: (row(b, i), g)),
        out_shape=jax.ShapeDtypeStruct((mp, QW), BF16),
        compiler_params=_cparams(("parallel", "parallel", "arbitrary")), name="nsa_prompt",
    )(slopes, proj, kcvc, kcvc, proj, proj, proj, proj, gates_t)


def _pool_sample_kernel(pt_ref, *refs, n_page):
    del pt_ref
    c_refs = refs[:n_page]
    wk_ref, wv_ref, ko_ref, vo_ref = refs[n_page:]
    per = c_refs[0].shape[0] // NSA_BLOCK
    for t, c_ref in enumerate(c_refs):
        for half in range(per):
            rows = pl.ds(half * NSA_BLOCK, NSA_BLOCK)
            ko_ref[t * per + half] = jnp.sum(c_ref[rows, 0] * wk_ref[...], axis=0)
            vo_ref[t * per + half] = jnp.sum(c_ref[rows, 1] * wv_ref[...], axis=0)


def _pool_sample(cache, e, page_table, wk, wv):
    nd, n_pages = page_table.shape
    page_rows = cache.shape[2]
    per = page_rows // NSA_BLOCK
    kvh, hd = cache.shape[4], cache.shape[5]
    n_page = math.gcd(n_pages, POOL_PAGES_PER_STEP)
    out = jax.ShapeDtypeStruct((nd, n_pages * per, kvh, hd), F32)
    page_spec = lambda t: pl.BlockSpec((None, None, page_rows, 2, kvh, hd),
                                       lambda b, p, pt: (e, pt[b, p * n_page + t], 0, 0, 0, 0))
    grid_spec = pltpu.PrefetchScalarGridSpec(
        num_scalar_prefetch=1, grid=(nd, n_pages // n_page),
        in_specs=[page_spec(t) for t in range(n_page)]
                 + [pl.BlockSpec((NSA_BLOCK, kvh, hd), lambda b, p, pt: (0, 0, 0))] * 2,
        out_specs=[pl.BlockSpec((None, n_page * per, kvh, hd), lambda b, p, pt: (b, p, 0, 0))] * 2)
    return pl.pallas_call(functools.partial(_pool_sample_kernel, n_page=n_page), grid_spec=grid_spec,
                          out_shape=[out, out],
                          compiler_params=_cparams(("parallel", "parallel")), name="nsa_pool_sample",
                          )(page_table, *([cache] * n_page), wk, wv)


def _head_rows(row, n):
    return jnp.concatenate([row[:, h * HEAD_DIM:(h + 1) * HEAD_DIM] for h in range(n)], axis=0)


def _nsa_cmp_sample_kernel(sl_ref, q_ref, kc_ref, vc_ref, oc_ref, oh_ref, *, past, n_sel):
    b = pl.program_id(0)
    nblk = kc_ref.shape[0]
    scale = HEAD_DIM ** -0.5
    rr = NSA_GROUP
    qh = _head_rows(q_ref[pl.ds(b, 1), :], NSA_HEADS).astype(BF16)
    blk = _iota((1, nblk), 1)
    dist = (past - (blk * NSA_BLOCK + (NSA_BLOCK - 1))).astype(F32)
    last = (past // NSA_BLOCK) - 1
    for g in range(NSA_KV_HEADS):
        qg = jnp.concatenate([qh[g * rr:(g + 1) * rr], jnp.zeros((8 - rr, HEAD_DIM), BF16)], axis=0)
        slope = jnp.concatenate([jnp.full((1, 1), sl_ref[g, r], F32) for r in range(rr)]
                                + [jnp.zeros((8 - rr, 1), F32)], axis=0)
        kc = kc_ref[:, g, :].astype(BF16)
        vc = vc_ref[:, g, :].astype(BF16)
        s = _dot_nt(qg, kc) * scale - slope * dist
        p = _msoftmax(s, blk >= 0)
        oc_ref[g * 8:(g + 1) * 8, :] = jnp.dot(p.astype(BF16), vc, preferred_element_type=F32)
        imp = jnp.sum(p[0:rr], axis=0, keepdims=True)
        imp = jnp.where((blk == 0) | (blk == last), NSA_FORCE, imp)
        impb = jnp.broadcast_to(imp, (nblk, nblk))
        eye = _iota((nblk, nblk), 0) == _iota((nblk, nblk), 1)
        impc = jnp.sum(jnp.where(eye, impb, 0.0), axis=1, keepdims=True)
        before = (impc > impb) | ((impc == impb) & (_iota((nblk, nblk), 0) < _iota((nblk, nblk), 1)))
        rank = jnp.sum(before.astype(F32), axis=0, keepdims=True)
        sel = rank < n_sel
        lower = (_iota((nblk, nblk), 0) < _iota((nblk, nblk), 1)).astype(BF16)
        slot = jnp.dot(jnp.broadcast_to(sel.astype(BF16), (8, nblk)), lower, preferred_element_type=F32)[0:1]
        onehot = sel & (jnp.abs(slot - _iota((NSA_TOPK, nblk), 0).astype(F32)) < 0.5)
        idx = jnp.sum(jnp.where(onehot, _iota((NSA_TOPK, nblk), 1), 0), axis=1, keepdims=True)
        oh_ref[g] = jnp.broadcast_to(idx, (NSA_TOPK, HEAD_DIM))


def _nsa_cmp_sample(proj, kcs, vcs, slopes, row0, past):
    nd, nblk = kcs.shape[0], kcs.shape[1]
    n_sel = min(NSA_TOPK, nblk + 1) - 1
    assert past % NSA_BLOCK == 0 and nblk >= NSA_TOPK
    rb = row0 // SAMPLE_PAD
    return pl.pallas_call(
        functools.partial(_nsa_cmp_sample_kernel, past=past, n_sel=n_sel),
        grid=(nd,),
        in_specs=[pl.BlockSpec(memory_space=pltpu.SMEM),
                  pl.BlockSpec((SAMPLE_PAD, QW), lambda b: (rb, 0)),
                  pl.BlockSpec((None, nblk, NSA_KV_HEADS, HEAD_DIM), lambda b: (b, 0, 0, 0)),
                  pl.BlockSpec((None, nblk, NSA_KV_HEADS, HEAD_DIM), lambda b: (b, 0, 0, 0))],
        out_specs=[pl.BlockSpec((None, 8 * NSA_KV_HEADS, HEAD_DIM), lambda b: (b, 0, 0)),
                   pl.BlockSpec((None, NSA_KV_HEADS, NSA_TOPK, HEAD_DIM), lambda b: (b, 0, 0, 0))],
        out_shape=[jax.ShapeDtypeStruct((nd, 8 * NSA_KV_HEADS, HEAD_DIM), F32),
                   jax.ShapeDtypeStruct((nd, NSA_KV_HEADS, NSA_TOPK, HEAD_DIM), I32)],
        compiler_params=_cparams(("parallel",)), name="nsa_cmp_sample",
    )(slopes, proj, kcs, vcs)


def _nsa_sel_sample_kernel(idx_ref, pt_ref, sl_ref, proj_ref, c0_ref, c1_ref, c2_ref, c3_ref, win_ref, oc_ref,
                           gt_ref, prev_ref, o_ref, m_ref, l_ref, acc_ref, rows_ref, *, past, n_sel, nd):
    del pt_ref, prev_ref
    b = pl.program_id(0)
    j = pl.program_id(1)
    scale = HEAD_DIM ** -0.5
    rr = NSA_GROUP
    c_refs = (c0_ref, c1_ref, c2_ref, c3_ref)

    @pl.when((b == 0) & (j == 0))
    def _():
        rows_ref[...] = jnp.zeros(rows_ref.shape, F32)

    @pl.when(j == 0)
    def _():
        m_ref[...] = jnp.full(m_ref.shape, NEG, F32)
        l_ref[...] = jnp.zeros(l_ref.shape, F32)
        acc_ref[...] = jnp.zeros(acc_ref.shape, F32)

    row = proj_ref[pl.ds(b, 1), :]
    qh = _head_rows(row[:, 0:QW], NSA_HEADS).astype(BF16)

    def group_q(g):
        qg = jnp.concatenate([qh[g * rr:(g + 1) * rr], jnp.zeros((8 - rr, HEAD_DIM), BF16)], axis=0)
        slope = jnp.concatenate([jnp.full((1, 1), sl_ref[g, r], F32) for r in range(rr)]
                                + [jnp.zeros((8 - rr, 1), F32)], axis=0)
        return qg, slope

    for g in range(NSA_KV_HEADS):
        qg, slope = group_q(g)
        blk_id = idx_ref[(b * NSA_KV_HEADS + g) * NSA_TOPK + j]
        kk = c_refs[g][:, 0, g, :].astype(BF16)
        vv = c_refs[g][:, 1, g, :].astype(BF16)
        dist = (past - (blk_id * NSA_BLOCK + _iota((1, NSA_BLOCK), 1))).astype(F32)
        sc = _dot_nt(qg, kk) * scale - slope * dist
        rows = slice(g * 8, (g + 1) * 8)
        m_old = m_ref[rows, 0:1]
        m_new = jnp.maximum(m_old, jnp.max(sc, axis=-1, keepdims=True))
        alpha = jnp.exp(m_old - m_new)
        e = jnp.exp(sc - m_new)
        l_ref[rows, :] = alpha * l_ref[rows, :] + jnp.sum(e, axis=-1, keepdims=True)
        acc_ref[rows, :] = alpha * acc_ref[rows, :] + jnp.dot(e.astype(BF16), vv, preferred_element_type=F32)
        m_ref[rows, :] = jnp.broadcast_to(m_new, (8, HEAD_DIM))

    @pl.when(j == n_sel - 1)
    def _():
        gt = _sigmoid(gt_ref[pl.ds(b, 1), :])
        pieces = []
        for g in range(NSA_KV_HEADS):
            qg, slope = group_q(g)
            qf = qg.astype(F32)
            base = QW + g * HEAD_DIM
            k_sel = row[:, base + 2 * KVW: base + 2 * KVW + HEAD_DIM].astype(BF16).astype(F32)
            v_sel = row[:, base + 3 * KVW: base + 3 * KVW + HEAD_DIM].astype(BF16).astype(F32)
            k_win = row[:, base + 4 * KVW: base + 4 * KVW + HEAD_DIM].astype(BF16).astype(F32)
            v_win = row[:, base + 5 * KVW: base + 5 * KVW + HEAD_DIM].astype(BF16).astype(F32)
            rows = slice(g * 8, (g + 1) * 8)
            s_new = jnp.sum(qf * k_sel, axis=-1, keepdims=True) * scale
            m_old = m_ref[rows, 0:1]
            m_new = jnp.maximum(m_old, s_new)
            alpha = jnp.exp(m_old - m_new)
            e_new = jnp.exp(s_new - m_new)
            l = alpha * l_ref[rows, 0:1] + e_new
            o_s = (alpha * acc_ref[rows, :] + e_new.astype(BF16).astype(F32) * v_sel) / jnp.maximum(l, 1e-30)
            nwin = win_ref.shape[0]
            kw = win_ref[:, 0, g, :].astype(BF16)
            vw = win_ref[:, 1, g, :].astype(BF16)
            dist = nwin - _iota((1, nwin), 1)
            sw = _dot_nt(qg, kw) * scale - slope * dist.astype(F32)
            valid = dist < NSA_WINDOW
            sw = jnp.where(valid, sw, NEG)
            sw_new = jnp.sum(qf * k_win, axis=-1, keepdims=True) * scale
            mw = jnp.maximum(jnp.max(sw, axis=-1, keepdims=True), sw_new)
            ew = jnp.where(valid, jnp.exp(sw - mw), 0.0)
            ew_new = jnp.exp(sw_new - mw)
            den = jnp.maximum(jnp.sum(ew, axis=-1, keepdims=True) + ew_new, 1e-30)
            o_w = (jnp.dot((ew / den).astype(BF16), vw, preferred_element_type=F32)
                   + (ew_new / den).astype(BF16).astype(F32) * v_win)
            o_c = oc_ref[rows, :]
            for r in range(rr):
                c = g * 3 * rr + 3 * r
                pieces.append(gt[:, c:c + 1] * o_c[r:r + 1] + gt[:, c + 1:c + 2] * o_s[r:r + 1]
                              + gt[:, c + 2:c + 3] * o_w[r:r + 1])
        rows_ref[pl.ds(b, 1), :] = jnp.concatenate(pieces, axis=1)

    @pl.when((b == nd - 1) & (j == n_sel - 1))
    def _():
        o_ref[...] = rows_ref[...].astype(o_ref.dtype)


def _nsa_sel_sample(proj, small, cache, win_cache, e, page_table, idx, o_cmp, slopes, prev, row0, past):
    nd = page_table.shape[0]
    page_rows = cache.shape[2]
    per = page_rows // NSA_BLOCK
    kvh, hd = cache.shape[4], cache.shape[5]
    nwin = win_cache.shape[2]
    n_sel = NSA_TOPK - 1
    rb = row0 // SAMPLE_PAD

    def cache_spec(g):
        def imap(b, j, idx_r, pt_r):
            blk = idx_r[(b * NSA_KV_HEADS + g) * NSA_TOPK + j]
            return (e, pt_r[b, blk // per], blk % per, 1, 0, 0)
        return pl.BlockSpec((None, None, NSA_BLOCK, 2, kvh, hd), imap)

    grid_spec = pltpu.PrefetchScalarGridSpec(
        num_scalar_prefetch=2, grid=(nd, n_sel),
        in_specs=[pl.BlockSpec(memory_space=pltpu.SMEM),
                  pl.BlockSpec((SAMPLE_PAD, proj.shape[1]), lambda b, j, i_r, p_r: (rb, 0)),
                  cache_spec(0), cache_spec(1), cache_spec(2), cache_spec(3),
                  pl.BlockSpec((None, None, nwin, 2, kvh, hd), lambda b, j, i_r, p_r: (e, b, 0, 0, 0, 0)),
                  pl.BlockSpec((None, 8 * NSA_KV_HEADS, HEAD_DIM), lambda b, j, i_r, p_r: (b, 0, 0)),
                  pl.BlockSpec((SAMPLE_PAD, small.shape[1]), lambda b, j, i_r, p_r: (rb, 0)),
                  pl.BlockSpec(memory_space=pl.ANY)],
        out_specs=pl.BlockSpec((SAMPLE_PAD, QW), lambda b, j, i_r, p_r: (rb, 0)),
        scratch_shapes=[pltpu.VMEM((8 * NSA_KV_HEADS, HEAD_DIM), F32)] * 3 + [pltpu.VMEM((SAMPLE_PAD, QW), F32)])
    return pl.pallas_call(
        functools.partial(_nsa_sel_sample_kernel, past=past, n_sel=n_sel, nd=nd),
        grid_spec=grid_spec,
        out_shape=jax.ShapeDtypeStruct(prev.shape, prev.dtype),
        input_output_aliases={11: 0},
        compiler_params=_cparams(("arbitrary", "arbitrary")), name="nsa_sel_sample",
    )(idx, page_table, slopes, proj, cache, cache, cache, cache, win_cache, o_cmp, small, prev)


def _stack_heads(x, n):
    return jnp.stack([x[:, h * HEAD_DIM:(h + 1) * HEAD_DIM] for h in range(n)], axis=0)


def _l2n(x):
    return x * lax.rsqrt(jnp.sum(x * x, axis=-1, keepdims=True) + EPS)


def _softplus(x):
    return jnp.maximum(x, 0.0) + jnp.log(1.0 + jnp.exp(-jnp.abs(x)))


def _bmm(a, b, **kw):
    return jnp.einsum('hcs,hsd->hcd', a, b, preferred_element_type=F32, **kw)


def _bmm_nt(a, b, **kw):
    return jnp.einsum('hcd,hsd->hcs', a, b, preferred_element_type=F32, **kw)


def _bmm_tn(a, b, **kw):
    return jnp.einsum('hcd,hce->hde', a, b, preferred_element_type=F32, **kw)


def _split_bf16(x):
    hi = x.astype(BF16)
    return hi, (x - hi.astype(F32)).astype(BF16)


def _bmm_split(ah, al, bh, bl):
    return _bmm(ah, bh) + (_bmm(ah, bl) + _bmm(al, bh))


def _gdn_prompt_kernel(x_ref, sm_ref, al_ref, dt_ref, z_ref, g_ref, o_ref, s_ref, *, a_col, b_col, batch):
    @pl.when(pl.program_id(0) == batch)
    def _():
        o_ref[...] = jnp.zeros(o_ref.shape, o_ref.dtype)

    @pl.when(pl.program_id(0) < batch)
    def _():
        _gdn_chunk(x_ref, sm_ref, al_ref, dt_ref, z_ref, g_ref, o_ref, s_ref, a_col=a_col, b_col=b_col)


def _gdn_chunk(x_ref, sm_ref, al_ref, dt_ref, z_ref, g_ref, o_ref, s_ref, *, a_col, b_col):
    nh, cc = GDN_HEADS, x_ref.shape[0]
    x = x_ref[...]
    q = _l2n(_stack_heads(x[:, 0:GH], nh)) * (HEAD_DIM ** -0.5)
    k = _l2n(_stack_heads(x[:, GH:2 * GH], nh))
    v = _stack_heads(x[:, 2 * GH:3 * GH], nh)
    sm = sm_ref[...]
    gdec = -jnp.exp(al_ref[...]) * _softplus(sm + dt_ref[...])
    ri, ci = _iota((cc, cc), 0), _iota((cc, cc), 1)
    tri = ri >= ci
    strict = ri > ci
    gcum = jnp.dot(tri.astype(F32), gdec, preferred_element_type=F32, precision=HIGHEST)
    beta = _sigmoid(sm)
    gc = jnp.stack([gcum[:, a_col + h:a_col + h + 1] for h in range(nh)], axis=0)
    bc = jnp.stack([beta[:, b_col + h:b_col + h + 1] for h in range(nh)], axis=0)
    gr = jnp.sum(jnp.where((ri == ci)[None], jnp.broadcast_to(gc, (nh, cc, cc)), 0.0), axis=1, keepdims=True)
    decay = jnp.where(tri[None], jnp.exp(jnp.where(tri[None], gc - gr, 0.0)), 0.0)
    kb = k * bc
    vb = v * bc
    a = jnp.where(strict[None], _bmm_nt(kb, k) * decay, 0.0)
    neg = -a
    tm = jnp.where((ri == ci)[None], 1.0, 0.0) + neg
    pw = neg
    span = 2
    while span < cc:
        ph, plo = _split_bf16(pw)
        pw = _bmm_split(ph, plo, ph, plo)
        ph, plo = _split_bf16(pw)
        th, tlo = _split_bf16(tm)
        tm = tm + _bmm_split(th, tlo, ph, plo)
        span *= 2
    eg = jnp.exp(gc)
    gl = gc[:, cc - 1:cc, :]
    u = _bmm(tm, vb)
    w = _bmm(tm, kb * eg)
    qk = jnp.where(tri[None], _bmm_nt(q, k) * decay, 0.0)
    qe = q * eg
    ke = k * jnp.exp(gl - gc)
    egl = jnp.exp(gl)

    @pl.when(pl.program_id(1) == 0)
    def _():
        s_ref[...] = jnp.zeros(s_ref.shape, F32)

    s = s_ref[...]
    v_new = u - _bmm(w, s)
    o = _bmm(qe, s) + _bmm(qk, v_new)
    s_ref[...] = s * egl + _bmm_tn(ke, v_new)
    z = _stack_heads(z_ref[...], nh)
    o = _rms(o) * g_ref[...][None] * (z * _sigmoid(z))
    for h in range(nh):
        o_ref[:, h * HEAD_DIM:(h + 1) * HEAD_DIM] = o[h].astype(o_ref.dtype)


def _gdn_prompt(qkv_c, small, a_log_row, dt_row, gz, gdn_g, mp, batch, seq, a_col, b_col):
    cc = GDN_CHUNK
    assert cc <= HEAD_DIM
    nch = seq // cc
    bb = lambda b: jnp.minimum(b, batch - 1)
    row = lambda b, n: jnp.where(b < batch, b * nch + n, batch * nch)
    vec = lambda w: pl.BlockSpec((1, w), lambda b, n: (0, 0))
    return pl.pallas_call(
        functools.partial(_gdn_prompt_kernel, a_col=a_col, b_col=b_col, batch=batch),
        grid=(batch + 1, nch),
        in_specs=[pl.BlockSpec((cc, 3 * GH), lambda b, n: (bb(b) * nch + n, 0)),
                  pl.BlockSpec((cc, small.shape[1]), lambda b, n: (bb(b) * nch + n, 0)),
                  vec(small.shape[1]), vec(small.shape[1]),
                  pl.BlockSpec((cc, GH), lambda b, n: (bb(b) * nch + n, 3)),
                  vec(HEAD_DIM)],
        out_specs=[pl.BlockSpec((cc, GH), lambda b, n: (row(b, n), 0)),
                   pl.BlockSpec((None, GDN_HEADS, HEAD_DIM, HEAD_DIM), lambda b, n: (bb(b), 0, 0, 0))],
        out_shape=[jax.ShapeDtypeStruct((mp, GH), BF16),
                   jax.ShapeDtypeStruct((batch, GDN_HEADS, HEAD_DIM, HEAD_DIM), F32)],
        compiler_params=_cparams(("parallel", "arbitrary")), name="gdn_prompt",
    )(qkv_c, small, a_log_row, dt_row, gz, gdn_g.reshape(1, HEAD_DIM))


def _col_of_row(row, n):
    eye = _iota((n, n), 0) == _iota((n, n), 1)
    return jnp.sum(jnp.where(eye, jnp.broadcast_to(row, (n, n)), 0.0), axis=1, keepdims=True)


def _gdn_sample_kernel(x_ref, h_ref, cw_ref, gz_ref, sm_ref, al_ref, dt_ref, g_ref, s0_ref, prev_ref,
                       o_ref, s_ref, rows_ref, *, a_col, b_col, nd):
    del prev_ref
    b = pl.program_id(0)
    nh = GDN_HEADS

    @pl.when(b == 0)
    def _():
        rows_ref[...] = jnp.zeros(rows_ref.shape, F32)

    nk = cw_ref.shape[0]
    conv = jnp.sum(h_ref[...] * cw_ref[0:nk - 1, :], axis=0, keepdims=True) + x_ref[pl.ds(b, 1), :] * cw_ref[nk - 1:nk, :]
    conv = conv * _sigmoid(conv)
    q = _l2n(_head_rows(conv[:, 0:GH], nh)) * (HEAD_DIM ** -0.5)
    k = _l2n(_head_rows(conv[:, GH:2 * GH], nh))
    v = _head_rows(conv[:, 2 * GH:3 * GH], nh)
    sm = sm_ref[pl.ds(b, 1), :]
    gdec = -jnp.exp(al_ref[...]) * _softplus(sm + dt_ref[...])
    eg = jnp.exp(_col_of_row(gdec[:, a_col:a_col + nh], nh))
    beta = _col_of_row(_sigmoid(sm)[:, b_col:b_col + nh], nh)
    s = s0_ref[...]
    pad = lambda t: jnp.concatenate([t[:, None, :], jnp.zeros((nh, 7, HEAD_DIM), F32)], axis=1)
    ks = _bmm(pad(k), s, precision=HIGHEST)[:, 0, :]
    v_new = beta * (v - eg * ks)
    o = eg * _bmm(pad(q), s, precision=HIGHEST)[:, 0, :] + jnp.sum(q * k, axis=-1, keepdims=True) * v_new
    s_ref[...] = s * eg[:, :, None] + _bmm_tn(pad(k), pad(v_new), precision=HIGHEST)
    z = _head_rows(gz_ref[pl.ds(b, 1), :], nh)
    o = _rms(o) * g_ref[...] * (z * _sigmoid(z))
    rows_ref[pl.ds(b, 1), :] = jnp.concatenate([o[h:h + 1] for h in range(nh)], axis=1)

    @pl.when(b == nd - 1)
    def _():
        o_ref[...] = rows_ref[...].astype(o_ref.dtype)


def _gdn_sample(gz, hist, conv_w, small, a_log_row, dt_row, gdn_g, s0, prev, row0, a_col, b_col):
    nd = hist.shape[0]
    rb = row0 // SAMPLE_PAD
    full = lambda shp: pl.BlockSpec(shp, lambda b: (0,) * len(shp))
    return pl.pallas_call(
        functools.partial(_gdn_sample_kernel, a_col=a_col, b_col=b_col, nd=nd),
        grid=(nd,),
        in_specs=[pl.BlockSpec((SAMPLE_PAD, 3 * GH), lambda b: (rb, 0)),
                  pl.BlockSpec((None, GDN_CONV - 1, 3 * GH), lambda b: (b, 0, 0)),
                  full(conv_w.shape),
                  pl.BlockSpec((SAMPLE_PAD, GH), lambda b: (rb, 3)),
                  pl.BlockSpec((SAMPLE_PAD, small.shape[1]), lambda b: (rb, 0)),
                  full(a_log_row.shape), full(dt_row.shape), full((1, HEAD_DIM)),
                  pl.BlockSpec((None, GDN_HEADS, HEAD_DIM, HEAD_DIM), lambda b: (b, 0, 0, 0)),
                  pl.BlockSpec(memory_space=pl.ANY)],
        out_specs=[pl.BlockSpec((SAMPLE_PAD, GH), lambda b: (rb, 0)),
                   pl.BlockSpec((None, GDN_HEADS, HEAD_DIM, HEAD_DIM), lambda b: (b, 0, 0, 0))],
        out_shape=[jax.ShapeDtypeStruct(prev.shape, prev.dtype), jax.ShapeDtypeStruct(s0.shape, F32)],
        scratch_shapes=[pltpu.VMEM((SAMPLE_PAD, GH), F32)],
        input_output_aliases={9: 0},
        compiler_params=_cparams(("arbitrary",)), name="gdn_sample",
    )(gz, hist, conv_w, gz, small, a_log_row, dt_row, gdn_g.reshape(1, HEAD_DIM), s0, prev)


def _even_layer(hn, e, dims, slopes, cache_nsa_kv, cache_nsa_win, state_gdn, state_gdn_conv, page_table,
                w_in, cmp_w, conv_w, a_log, dt_bias, gdn_g, w_out):
    batch, seq, nd, d = dims
    rows_p = batch * seq
    mp = hn.shape[0]
    past = page_table.shape[1] * cache_nsa_kv.shape[2]
    c_nsa = QW + 6 * KVW
    c_gate = c_nsa
    c_gdn = c_gate + 3 * NSA_HEADS
    c_a = c_gdn + 4 * GH
    a_col, b_col = 3 * NSA_HEADS, 3 * NSA_HEADS + GDN_HEADS
    hd = HEAD_DIM

    proj = _matmul([hn], [(w_in, e, d, 0, 0)], (0,), _ep_plain, c_nsa, F32, _col_tile(c_nsa, 512), "proj_nsa")
    tn = _col_tile(4 * GH, 512)
    off = c_gdn - c_nsa
    assert c_nsa % tn == 0 and 0 < off < hd and c_a % hd == off
    gdn_pieces = [(tn, c_nsa // tn, 1, off, tn), (hd, (c_nsa + tn) // hd, tn // hd, 0, off)]
    gz = _matmul([hn], [(w_in, e, d, 0, None, gdn_pieces)], (0,), _ep_plain, 4 * GH, F32, tn, "proj_gdn")
    small_pieces = [(hd, c_nsa // hd, 0, 0, off), (hd, c_a // hd, 0, off, off + 2 * GDN_HEADS)]
    small = _matmul([hn], [(w_in, e, d, 0, None, small_pieces)], (0,), _ep_plain, hd, F32, hd, "proj_small")

    w_exp = jnp.concatenate([jnp.repeat(cmp_w[:, :, 0], hd, axis=1), jnp.repeat(cmp_w[:, :, 1], hd, axis=1)], axis=1)
    kcvc = _pool_prompt(proj, w_exp, batch, seq)
    gates_t = small[:, 0:3 * NSA_HEADS].reshape(mp, NSA_KV_HEADS, 3 * NSA_GROUP).transpose(1, 0, 2)
    o_nsa = _nsa_prompt(proj, kcvc, gates_t, slopes, batch, seq)
    wk = jnp.broadcast_to(cmp_w[:, :, 0][:, :, None], (NSA_BLOCK, NSA_KV_HEADS, hd))
    wv = jnp.broadcast_to(cmp_w[:, :, 1][:, :, None], (NSA_BLOCK, NSA_KV_HEADS, hd))
    kcs, vcs = _pool_sample(cache_nsa_kv, e, page_table, wk, wv)
    o_cmp, sel_rows = _nsa_cmp_sample(proj, kcs, vcs, slopes, rows_p, past)
    idx = sel_rows[:, :, :, 0].reshape(-1)
    o_nsa = _nsa_sel_sample(proj, small, cache_nsa_kv, cache_nsa_win, e, page_table, idx, o_cmp, slopes, o_nsa,
                            rows_p, past)

    pad_row = lambda vec, col: jnp.zeros((1, hd), F32).at[0, col:col + GDN_HEADS].set(vec.astype(F32))
    a_log_row, dt_row = pad_row(a_log, a_col), pad_row(dt_bias, a_col)
    qkv_c = _dwconv_prompt(gz, conv_w, jnp.zeros((3 * GH,), F32), 0, 3 * GH, batch, seq, True, "gdn_conv_prompt")
    o_gdn, s_prompt = _gdn_prompt(qkv_c, small, a_log_row, dt_row, gz, gdn_g, mp, batch, seq, a_col, b_col)
    o_gdn, s_sample = _gdn_sample(gz, state_gdn_conv[e], conv_w, small, a_log_row, dt_row, gdn_g, state_gdn[e], o_gdn,
                                  rows_p, a_col, b_col)

    mixed = _matmul([o_nsa, o_gdn], [(w_out, e, QW, 0, 0), (w_out, e, GH, 1, 0)], (0, 1), _ep_sum, d, F32,
                    _col_tile(d, 512), "proj_out")

    kv_rows = proj[:, QW:QW + 4 * KVW]
    win_rows = proj[:, QW + 4 * KVW:QW + 6 * KVW]
    kv_p = kv_rows[:rows_p].reshape(batch, seq, 4, NSA_KV_HEADS, hd)
    kv_s = kv_rows[rows_p:rows_p + nd].reshape(nd, 1, 4, NSA_KV_HEADS, hd)
    nwin_p = min(NSA_WINDOW, seq)
    win_p = jnp.stack([win_rows[(b + 1) * seq - nwin_p:(b + 1) * seq] for b in range(batch)])
    win_p = win_p.reshape(batch, nwin_p, 2, NSA_KV_HEADS, hd)
    win_all = jnp.concatenate([cache_nsa_win[e], win_rows[rows_p:rows_p + nd].reshape(nd, 1, 2, NSA_KV_HEADS, hd)], axis=1)
    win_s = win_all[:, win_all.shape[1] - min(NSA_WINDOW, win_all.shape[1]):]
    gconv_p = jnp.stack([gz[(b + 1) * seq - (GDN_CONV - 1):(b + 1) * seq, 0:3 * GH] for b in range(batch)])
    gconv_s = jnp.concatenate([state_gdn_conv[e], gz[rows_p:rows_p + nd, 0:3 * GH][:, None]], axis=1)[:, 1:]
    return mixed, (kv_p, kv_s, win_p, win_s, s_prompt, s_sample, gconv_p, gconv_s)


def _odd_layer(hn, o, dims, state_conf_conv, w_pw1, w_dw, b_dw, ln_g, ln_b, w_pw2):
    batch, seq, nd, d = dims
    rows_p = batch * seq
    ch = w_dw.shape[1]
    tn = _col_tile(ch, 256)
    u = _matmul([hn], [(w_pw1, o, d, 0, 0), (w_pw1, o, d, 0, ch // tn)], (0, 0), _ep_glu, ch, F32, tn, "conf_pw1_glu")
    c = _dwconv_prompt(u, w_dw, b_dw, 0, ch, batch, seq, False, "conf_conv_prompt")
    c = _dwconv_sample(u, state_conf_conv[o], w_dw, b_dw, 0, ch, c, rows_p, False, "conf_conv_sample")
    y = _ln_silu(c, ln_g, ln_b)
    mixed = _matmul([y], [(w_pw2, o, ch, 0, 0)], (0,), _ep_plain, d, F32, _col_tile(d, 512), "conf_pw2")
    cc_p = jnp.stack([u[(b + 1) * seq - (CONF_WIDTH - 1):(b + 1) * seq] for b in range(batch)])
    cc_s = jnp.concatenate([state_conf_conv[o], u[rows_p:rows_p + nd][:, None]], axis=1)[:, 1:]
    return mixed, (cc_p, cc_s)


def kernel(x_prompt, x_sample, cache_nsa_kv, cache_nsa_win, state_gdn, state_gdn_conv, state_conf_conv, page_table,
           p_prompt, p_sample, norm_g, w_in_even, cmp_pool_w, gdn_conv_w, gdn_a_log, gdn_dt_bias, gdn_norm_g,
           w_out_even, w_pw1, w_dw, b_dw, conf_ln_g, conf_ln_b, w_pw2, w_ffn_gate, w_ffn_up, w_ffn_down,
           w_ple_gate, w_ple_proj):
    batch, seq, d = x_prompt.shape
    nd = x_sample.shape[0]
    depth = norm_g.shape[0]
    ffn = w_ffn_gate.shape[2]
    ple = w_ple_proj.shape[1]
    rows_p = batch * seq
    assert x_sample.shape[1] == 1 and nd <= 8 and rows_p % SAMPLE_PAD == 0
    dims = (batch, seq, nd, d)
    slopes = jnp.asarray(np.asarray([2.0 ** (-8.0 * (h + 1) / NSA_HEADS) for h in range(NSA_HEADS)], np.float32)
                         .reshape(NSA_KV_HEADS, NSA_GROUP))

    padrows = lambda a: jnp.concatenate([a, jnp.zeros((SAMPLE_PAD - nd,) + a.shape[1:], a.dtype)], axis=0)
    h = jnp.concatenate([x_prompt.reshape(rows_p, d), padrows(x_sample.reshape(nd, d))], axis=0)
    ones = jnp.ones((d,), F32)
    tn_f = _col_tile(ffn, 256)
    w_down_bf16 = w_ffn_down.astype(BF16)
    outs = [[] for _ in range(10)]
    hn = _norm(h, norm_g[0, 0])
    for i in range(depth):
        g_post, f_pre, f_post = norm_g[i, 1], norm_g[i, 2], norm_g[i, 3]
        if i % 2 == 0:
            e = i // 2
            mixed, st = _even_layer(hn, e, dims, slopes, cache_nsa_kv, cache_nsa_win, state_gdn, state_gdn_conv,
                                    page_table, w_in_even, cmp_pool_w[e], gdn_conv_w[e], gdn_a_log[e],
                                    gdn_dt_bias[e], gdn_norm_g[e], w_out_even)
            for lst, val in zip(outs[:8], st):
                lst.append(val)
        else:
            o = i // 2
            mixed, st = _odd_layer(hn, o, dims, state_conf_conv, w_pw1, w_dw[o], b_dw[o], conf_ln_g[o],
                                   conf_ln_b[o], w_pw2)
            outs[8].append(st[0])
            outs[9].append(st[1])
        h, hn = _addnorm(h, mixed, g_post, f_pre)
        act = _matmul([hn], [(w_ffn_gate, i, d, 0, 0), (w_ffn_up, i, d, 0, 0)], (0, 0), _ep_swiglu, ffn, BF16, tn_f,
                      "ffn_gate_up")
        f = _matmul([act], [(w_down_bf16, i, ffn, 0, 0)], (0,), _ep_plain, d, F32, _col_tile(d, 512), "ffn_down",
                    x_stationary=True)
        h, hn = _addnorm(h, f, f_post, ones)
        p_i = jnp.concatenate([p_prompt[i].reshape(rows_p, ple), padrows(p_sample[i].reshape(nd, ple))], axis=0).astype(BF16)
        h = _matmul([hn, p_i], [(w_ple_gate, i, d, 0, 0), (w_ple_proj, i, ple, 0, 0)], (0, 1), _ep_ple, d, F32,
                    _col_tile(d, 512), "ple", res=h)
        if i + 1 < depth:
            hn = _norm(h, norm_g[i + 1, 0])
    y_p = h[:rows_p].reshape(batch, seq, d)
    y_s = h[rows_p:rows_p + nd].reshape(nd, 1, d)
    return (y_p, y_s) + tuple(jnp.stack(lst) for lst in outs)
```

```python
import functools
import math

import numpy as np
import jax
import jax.numpy as jnp
from jax import lax
from jax.experimental import pallas as pl
from jax.experimental.pallas import tpu as pltpu

F32 = jnp.float32
BF16 = jnp.bfloat16
I32 = jnp.int32
HIGHEST = lax.Precision.HIGHEST

HEAD_DIM = 128
NSA_HEADS = 16
NSA_KV_HEADS = 4
NSA_GROUP = NSA_HEADS // NSA_KV_HEADS
NSA_BLOCK = 64
NSA_TOPK = 16
NSA_WINDOW = 512
NSA_FORCE = 1.0e4
GDN_HEADS = 16
GDN_CONV = 4
GDN_CHUNK = 64
CONF_WIDTH = 31
EPS = 1e-6
NEG = -1.0e30

SAMPLE_PAD = 16
V7X_VMEM_LIMIT = 58 * 1024 * 1024
MM_VMEM_BUDGET = 50 * 1024 * 1024
MM_MAX_ROWS = 1024
NSA_QTILE = 256
SEL_KCHUNK = 512
ROW_TILE_ROWS = 320
POOL_PAGES_PER_STEP = 8

QW = NSA_HEADS * HEAD_DIM
GW = NSA_GROUP * HEAD_DIM
KVW = NSA_KV_HEADS * HEAD_DIM
GH = GDN_HEADS * HEAD_DIM


def _cparams(sem):
    return pltpu.CompilerParams(dimension_semantics=sem, vmem_limit_bytes=V7X_VMEM_LIMIT)


def _divisor_tile(n, target, mult):
    best = None
    for t in range(mult, min(n, target) + 1, mult):
        if n % t == 0:
            best = t
    assert best is not None, (n, target, mult)
    return best


def _sigmoid(x):
    return 1.0 / (1.0 + jnp.exp(-x))


def _iota(shape, dim):
    return lax.broadcasted_iota(I32, shape, dim)


def _rms(x):
    return x * lax.rsqrt(jnp.mean(x * x, axis=-1, keepdims=True) + EPS)


def _msoftmax(s, mask):
    s = jnp.where(mask, s, NEG)
    m = jnp.max(s, axis=-1, keepdims=True)
    e = jnp.where(mask, jnp.exp(s - m), 0.0)
    return e / jnp.maximum(jnp.sum(e, axis=-1, keepdims=True), 1e-30)


def _dot_nt(a, b, **kw):
    return lax.dot_general(a, b, (((1,), (1,)), ((), ())), preferred_element_type=F32, **kw)


def _norm_kernel(h_ref, g_ref, o_ref):
    o_ref[...] = (_rms(h_ref[...]) * g_ref[...]).astype(o_ref.dtype)


def _addnorm_kernel(h_ref, f_ref, g1_ref, g2_ref, h2_ref, hn_ref):
    h2 = h_ref[...] + _rms(f_ref[...]) * g1_ref[...]
    h2_ref[...] = h2
    hn_ref[...] = (_rms(h2) * g2_ref[...]).astype(hn_ref.dtype)


def _ln_silu_kernel(c_ref, g_ref, b_ref, o_ref):
    x = c_ref[...]
    xc = x - jnp.mean(x, axis=-1, keepdims=True)
    y = xc * lax.rsqrt(jnp.mean(xc * xc, axis=-1, keepdims=True) + EPS)
    y = y * g_ref[...] + b_ref[...]
    o_ref[...] = (y * _sigmoid(y)).astype(o_ref.dtype)


def _row_call(kernel, mats, vecs, out_dtypes, name):
    mp, d = mats[0].shape
    tr = _divisor_tile(mp, ROW_TILE_ROWS, SAMPLE_PAD)
    mat_spec = pl.BlockSpec((tr, d), lambda i: (i, 0))
    vec_spec = pl.BlockSpec((1, d), lambda i: (0, 0))
    outs = [jax.ShapeDtypeStruct((mp, d), dt) for dt in out_dtypes]
    res = pl.pallas_call(
        kernel, grid=(mp // tr,),
        in_specs=[mat_spec] * len(mats) + [vec_spec] * len(vecs),
        out_specs=[mat_spec] * len(outs), out_shape=outs,
        compiler_params=_cparams(("parallel",)), name=name,
    )(*mats, *[v.reshape(1, d).astype(F32) for v in vecs])
    return res


def _norm(h, g):
    return _row_call(_norm_kernel, [h], [g], [BF16], "rmsnorm")[0]


def _addnorm(h, f, g1, g2):
    return _row_call(_addnorm_kernel, [h, f], [g1, g2], [F32, BF16], "add_rmsnorm")


def _ln_silu(c, g, b):
    return _row_call(_ln_silu_kernel, [c], [g, b], [BF16], "layernorm_silu")[0]


def _ep_plain(accs, res):
    return accs[0]


def _ep_sum(accs, res):
    return accs[0] + accs[1]


def _ep_swiglu(accs, res):
    return accs[0] * _sigmoid(accs[0]) * accs[1]


def _ep_glu(accs, res):
    return accs[0] * _sigmoid(accs[1])


def _ep_ple(accs, res):
    return res + _sigmoid(accs[0]) * accs[1]


def _w_pieces(wspec, tn):
    return wspec[5] if len(wspec) == 6 else [(tn, wspec[4], 1, 0, tn)]


def _mm_kernel(*refs, x_of_w, pieces, epilogue, has_res, cast):
    nx = max(x_of_w) + 1
    n_piece = [len(p) for p in pieces]
    x_refs = refs[:nx]
    pos = nx
    w_refs = []
    for n in n_piece:
        w_refs.append(refs[pos:pos + n])
        pos += n
    res_ref = refs[pos] if has_res else None
    pos += int(has_res)
    o_ref = refs[pos]
    wb_refs = refs[pos + 1:] if cast else [r[0] for r in w_refs]

    if cast:
        @pl.when(pl.program_id(1) == 0)
        def _():
            for prefs, pcs, wb_ref in zip(w_refs, pieces, wb_refs):
                parts = [r[...][:, l0:l1] if (l0, l1) != (0, bw) else r[...] for r, (bw, _, _, l0, l1) in zip(prefs, pcs)]
                fill = wb_ref.shape[1] - sum(l1 - l0 for (_, _, _, l0, l1) in pcs)
                if fill:
                    parts.append(jnp.zeros((wb_ref.shape[0], fill), F32))
                tile = parts[0] if len(parts) == 1 else jnp.concatenate(parts, axis=1)
                wb_ref[...] = tile.astype(BF16)

    accs = [jnp.dot(x_refs[xi][...], wb[...], preferred_element_type=F32) for xi, wb in zip(x_of_w, wb_refs)]
    res = res_ref[...] if has_res else None
    o_ref[...] = epilogue(accs, res).astype(o_ref.dtype)


def _mm_row_tile(mp, xs, ws, tn, out_dtype, has_res, x_stationary):
    fixed = 0
    for wspec in ws:
        w, k = wspec[0], wspec[2]
        fixed += 2 * k * sum(p[0] for p in _w_pieces(wspec, tn)) * w.dtype.itemsize
        fixed += k * tn * 2 if w.dtype != BF16 else 0
    per_row = sum(x.shape[1] * x.dtype.itemsize * (1 if x_stationary else 2) for x in xs)
    per_row += 2 * tn * jnp.dtype(out_dtype).itemsize + (2 * tn * 4 if has_res else 0)
    per_row += len(ws) * tn * 4
    best = SAMPLE_PAD
    for tm in range(SAMPLE_PAD, MM_MAX_ROWS + 1, SAMPLE_PAD):
        if mp % tm == 0 and fixed + tm * per_row <= MM_VMEM_BUDGET:
            best = tm
    return best


def _matmul(xs, ws, x_of_w, epilogue, n_out, out_dtype, tn, name, res=None, x_stationary=False):
    mp = xs[0].shape[0]
    assert n_out % tn == 0
    cast = not x_stationary
    assert all((wspec[0].dtype == BF16) == x_stationary for wspec in ws)
    tm = _mm_row_tile(mp, xs, ws, tn, out_dtype, res is not None, x_stationary)
    if x_stationary:
        grid = (mp // tm, n_out // tn)
        ij = lambda a, b: (a, b)
    else:
        grid = (n_out // tn, mp // tm)
        ij = lambda a, b: (b, a)
    x_mode = dict(pipeline_mode=pl.Buffered(1)) if x_stationary else {}
    in_specs = [pl.BlockSpec((tm, x.shape[1]), lambda a, b: (ij(a, b)[0], 0), **x_mode) for x in xs]
    arrs = list(xs)
    scratch = []
    pieces = [_w_pieces(wspec, tn) for wspec in ws]
    for wspec, pcs in zip(ws, pieces):
        w, layer, k, rb = wspec[:4]
        assert len(pcs) == 1 or cast
        for (bw, c0, step, _, _) in pcs:
            in_specs.append(pl.BlockSpec((None, k, bw), functools.partial(
                lambda a, b, layer, rb, c0, step: (layer, rb, c0 + step * ij(a, b)[1]), layer=layer, rb=rb, c0=c0, step=step)))
            arrs.append(w)
        if cast:
            scratch.append(pltpu.VMEM((k, tn), BF16))
    if res is not None:
        in_specs.append(pl.BlockSpec((tm, tn), lambda a, b: ij(a, b)))
        arrs.append(res)
    return pl.pallas_call(
        functools.partial(_mm_kernel, x_of_w=tuple(x_of_w), pieces=pieces, epilogue=epilogue, has_res=res is not None,
                          cast=cast),
        grid=grid,
        in_specs=in_specs,
        out_specs=pl.BlockSpec((tm, tn), lambda a, b: ij(a, b)),
        out_shape=jax.ShapeDtypeStruct((mp, n_out), out_dtype),
        scratch_shapes=scratch,
        compiler_params=_cparams(("arbitrary", "arbitrary")), name=name,
    )(*arrs)


def _col_tile(n, target):
    return _divisor_tile(n, target, 128)


def _dwconv_prompt_kernel(x_ref, w_ref, b_ref, o_ref, xs_ref, wb_ref, *, width, seq, pad, rows, act, batch):
    ct = x_ref.shape[1]

    @pl.when(pl.program_id(0) == batch)
    def _():
        o_ref[...] = jnp.zeros(o_ref.shape, F32)

    @pl.when(pl.program_id(0) < batch)
    def _():
        _dwconv_prompt_body(x_ref, w_ref, b_ref, o_ref, xs_ref, wb_ref, width=width, seq=seq, pad=pad, rows=rows,
                            act=act)


def _dwconv_prompt_body(x_ref, w_ref, b_ref, o_ref, xs_ref, wb_ref, *, width, seq, pad, rows, act):
    ct = x_ref.shape[1]
    xs_ref[0:pad, :] = jnp.zeros((pad, ct), F32)
    xs_ref[pad:pad + seq, :] = x_ref[...]
    bias = b_ref[...]
    for j in range(width):
        wb_ref[j] = jnp.broadcast_to(w_ref[pl.ds(j, 1), :], (8, ct))

    def body(i, carry):
        r0 = pl.multiple_of(i * rows, rows)
        acc = jnp.zeros((rows, ct), F32) + bias
        win = xs_ref[pl.ds(r0, rows + pad), :]
        off = pad - (width - 1)
        for r in range(8):
            taps = [j for j in range(width) if (off + j) % 8 == r]
            if not taps:
                continue
            wr = win if r == 0 else pltpu.roll(win, rows + pad - r, 0)
            for j in taps:
                base = off + j - r
                acc = acc + jnp.concatenate([wb_ref[j]] * (rows // 8), axis=0) * wr[base:base + rows]
        if act:
            acc = acc * _sigmoid(acc)
        o_ref[pl.ds(r0, rows), :] = acc
        return carry

    lax.fori_loop(0, seq // rows, body, 0)


def _dwconv_prompt(x, w, bias, col0, ncols, batch, seq, act, name):
    mp = x.shape[0]
    width = w.shape[0]
    ct = _col_tile(ncols, 512)
    assert col0 % ct == 0
    pad = -(-(width - 1) // 8) * 8
    rows = 32
    return pl.pallas_call(
        functools.partial(_dwconv_prompt_kernel, width=width, seq=seq, pad=pad, rows=rows, act=act, batch=batch),
        grid=(batch + 1, ncols // ct),
        in_specs=[pl.BlockSpec((seq, ct), lambda b, j: (jnp.minimum(b, batch - 1), col0 // ct + j)),
                  pl.BlockSpec((width, ct), lambda b, j: (0, j)),
                  pl.BlockSpec((1, ct), lambda b, j: (0, j))],
        out_specs=pl.BlockSpec((seq, ct), lambda b, j: (b, j)),
        out_shape=jax.ShapeDtypeStruct((mp, ncols), F32),
        scratch_shapes=[pltpu.VMEM((pad + seq, ct), F32), pltpu.VMEM((width, 8, ct), F32)],
        compiler_params=_cparams(("parallel", "parallel")), name=name,
    )(x, w, bias.reshape(1, ncols))


def _dwconv_sample_kernel(x_ref, h_ref, w_ref, b_ref, prev_ref, o_ref, *, width, nb, act):
    del prev_ref
    ct = x_ref.shape[1]
    acc = jnp.sum(h_ref[...] * w_ref[0:width - 1, :][None], axis=1)
    acc = acc + x_ref[0:nb, :] * w_ref[width - 1:width, :] + b_ref[...]
    if act:
        acc = acc * _sigmoid(acc)
    o_ref[0:nb, :] = acc
    o_ref[nb:, :] = jnp.zeros((SAMPLE_PAD - nb, ct), F32)


def _dwconv_sample(x, hist, w, bias, col0, ncols, prev, row0, act, name):
    nb = hist.shape[0]
    width = w.shape[0]
    ct = _col_tile(ncols, 512)
    rb = row0 // SAMPLE_PAD
    return pl.pallas_call(
        functools.partial(_dwconv_sample_kernel, width=width, nb=nb, act=act),
        grid=(ncols // ct,),
        in_specs=[pl.BlockSpec((SAMPLE_PAD, ct), lambda j: (rb, col0 // ct + j)),
                  pl.BlockSpec((nb, width - 1, ct), lambda j: (0, 0, j)),
                  pl.BlockSpec((width, ct), lambda j: (0, j)),
                  pl.BlockSpec((1, ct), lambda j: (0, j)),
                  pl.BlockSpec(memory_space=pl.ANY)],
        out_specs=pl.BlockSpec((SAMPLE_PAD, ct), lambda j: (rb, j)),
        out_shape=jax.ShapeDtypeStruct(prev.shape, F32),
        input_output_aliases={4: 0},
        compiler_params=_cparams(("parallel",)), name=name,
    )(x, hist, w, bias.reshape(1, ncols), prev)


def _pool_prompt_kernel(x_ref, w_ref, o_ref, *, nblk):
    x = x_ref[...].reshape(nblk, NSA_BLOCK, x_ref.shape[1])
    o_ref[...] = jnp.sum(x * w_ref[...][None], axis=1)


def _pool_prompt(proj, w_exp, batch, seq):
    nblk = seq // NSA_BLOCK
    return pl.pallas_call(
        functools.partial(_pool_prompt_kernel, nblk=nblk),
        grid=(batch,),
        in_specs=[pl.BlockSpec((seq, 2 * KVW), lambda b: (b, QW // (2 * KVW))),
                  pl.BlockSpec((NSA_BLOCK, 2 * KVW), lambda b: (0, 0))],
        out_specs=pl.BlockSpec((None, nblk, 2 * KVW), lambda b: (b, 0, 0)),
        out_shape=jax.ShapeDtypeStruct((batch, nblk, 2 * KVW), F32),
        compiler_params=_cparams(("parallel",)), name="nsa_pool_prompt",
    )(proj, w_exp)


def _nsa_prompt_kernel(sl_ref, q_ref, kc_ref, vc_ref, ks_ref, vs_ref, kw_ref, vw_ref, gt_ref, o_ref, *, tq, seq, batch):
    @pl.when(pl.program_id(0) == batch)
    def _():
        o_ref[...] = jnp.zeros(o_ref.shape, o_ref.dtype)

    @pl.when(pl.program_id(0) < batch)
    def _():
        _nsa_prompt_body(sl_ref, q_ref, kc_ref, vc_ref, ks_ref, vs_ref, kw_ref, vw_ref, gt_ref, o_ref, tq=tq, seq=seq)


def _nsa_prompt_body(sl_ref, q_ref, kc_ref, vc_ref, ks_ref, vs_ref, kw_ref, vw_ref, gt_ref, o_ref, *, tq, seq):
    g = pl.program_id(1)
    p0 = pl.program_id(2) * tq
    nblk = seq // NSA_BLOCK
    n_keep = min(NSA_TOPK, nblk)
    scale = HEAD_DIM ** -0.5
    rr = NSA_GROUP
    slopes = [sl_ref[g, r] for r in range(rr)]
    heads = [slice(r * tq, (r + 1) * tq) for r in range(rr)]
    tn_dims = (((0,), (0,)), ((), ()))

    q = q_ref[...]
    qs = jnp.concatenate([q[:, r * HEAD_DIM:(r + 1) * HEAD_DIM] for r in range(rr)], axis=0).astype(BF16)
    pos = p0 + _iota((tq, 1), 0)

    kc = kc_ref[...].astype(BF16)
    vc = vc_ref[...].astype(BF16)
    pos_l = p0 + _iota((1, tq), 1)
    blk = _iota((nblk, 1), 0)
    blk_end = blk * NSA_BLOCK + (NSA_BLOCK - 1)
    valid_c = blk_end <= pos_l
    dist_c = (pos_l - blk_end).astype(F32)
    st = _dot_nt(kc, qs) * scale
    probs = []
    for r in range(rr):
        sr = jnp.where(valid_c, st[:, heads[r]] - slopes[r] * dist_c, NEG)
        e = jnp.where(valid_c, jnp.exp(sr - jnp.max(sr, axis=0, keepdims=True)), 0.0)
        probs.append(e / jnp.maximum(jnp.sum(e, axis=0, keepdims=True), 1e-30))
    pt = jnp.concatenate(probs, axis=1)
    o_c = lax.dot_general(pt.astype(BF16), vc, tn_dims, preferred_element_type=F32)

    imp = probs[0]
    for r in range(1, rr):
        imp = imp + probs[r]
    cur = pos_l // NSA_BLOCK
    forced = (blk == 0) | (blk == cur) | (blk == cur - 1)
    imp = jnp.where(forced, NSA_FORCE, imp)
    imp = jnp.where(blk * NSA_BLOCK > pos_l, -1.0, imp)
    rank = jnp.zeros((nblk, tq), I32)
    for i in range(nblk):
        c = imp[i:i + 1, :]
        rank = rank + ((c > imp) | ((c == imp) & (blk > i))).astype(I32)
    sel_t = (rank < n_keep).astype(BF16)

    kch = min(SEL_KCHUNK, seq)
    n_chunks = (p0 + tq + kch - 1) // kch

    def sel_body(c, carry):
        m, l, acc = carry
        k0 = pl.multiple_of(c * kch, kch)
        kk = ks_ref[pl.ds(k0, kch), :].astype(BF16)
        vv = vs_ref[pl.ds(k0, kch), :].astype(BF16)
        kpos = k0 + _iota((1, kch), 1)
        expand = (_iota((nblk, kch), 0) == ((k0 + _iota((nblk, kch), 1)) // NSA_BLOCK)).astype(BF16)
        selk = lax.dot_general(sel_t, expand, tn_dims, preferred_element_type=F32) > 0.5
        mask = selk & (kpos <= pos)
        kb = (kpos - p0).astype(F32)
        qk = _dot_nt(qs, kk) * scale
        sc = jnp.concatenate([jnp.where(mask, qk[heads[r]] + slopes[r] * kb, NEG) for r in range(rr)], axis=0)
        m_new = jnp.maximum(m, jnp.max(sc, axis=-1, keepdims=True))
        alpha = jnp.exp(m - m_new)
        e = jnp.exp(sc - m_new)
        l = alpha * l + jnp.sum(e, axis=-1, keepdims=True)
        acc = alpha * acc + jnp.dot(e.astype(BF16), vv, preferred_element_type=F32)
        return m_new, l, acc

    init = (jnp.full((rr * tq, 1), NEG, F32), jnp.zeros((rr * tq, 1), F32), jnp.zeros((rr * tq, HEAD_DIM), F32))
    _, l, acc = lax.fori_loop(0, n_chunks, sel_body, init)
    o_s = acc / jnp.maximum(l, 1e-30)

    span = min(NSA_WINDOW + tq, seq)
    w0 = pl.multiple_of(jnp.clip(p0 + tq - span, 0, seq - span), 128)
    kk = kw_ref[pl.ds(w0, span), :].astype(BF16)
    vv = vw_ref[pl.ds(w0, span), :].astype(BF16)
    kpos = w0 + _iota((1, span), 1)
    dist = pos - kpos
    valid_w = (dist >= 0) & (dist < NSA_WINDOW)
    kb = (kpos - p0).astype(F32)
    qk = _dot_nt(qs, kk) * scale
    ews, lws = [], []
    for r in range(rr):
        sr = jnp.where(valid_w, qk[heads[r]] + slopes[r] * kb, NEG)
        e = jnp.exp(sr - jnp.max(sr, axis=-1, keepdims=True))
        ews.append(e)
        lws.append(jnp.sum(e, axis=-1, keepdims=True))
    o_w = jnp.dot(jnp.concatenate(ews, axis=0).astype(BF16), vv, preferred_element_type=F32)

    gt = _sigmoid(gt_ref[...])
    for r in range(rr):
        o_ref[:, r * HEAD_DIM:(r + 1) * HEAD_DIM] = (gt[:, 3 * r:3 * r + 1] * o_c[heads[r]]
                                                    + gt[:, 3 * r + 1:3 * r + 2] * o_s[heads[r]]
                                                    + gt[:, 3 * r + 2:3 * r + 3] * (o_w[heads[r]] / lws[r])
                                                    ).astype(o_ref.dtype)


def _nsa_prompt(proj, kcvc, gates_t, slopes, batch, seq):
    mp = proj.shape[0]
    tq = min(NSA_QTILE, seq)
    nq = seq // tq
    nblk = seq // NSA_BLOCK
    hb = HEAD_DIM
    c_sel = (QW + 2 * KVW) // hb
    c_win = (QW + 4 * KVW) // hb
    g4 = NSA_KV_HEADS
    bb = lambda b: jnp.minimum(b, batch - 1)
    row = lambda b, i: jnp.where(b < batch, b * nq + i, batch * nq)
    return pl.pallas_call(
        functools.partial(_nsa_prompt_kernel, tq=tq, seq=seq, batch=batch),
        grid=(batch + 1, NSA_KV_HEADS, nq),
        in_specs=[pl.BlockSpec(memory_space=pltpu.SMEM),
                  pl.BlockSpec((tq, GW), lambda b, g, i: (bb(b) * nq + i, g)),
                  pl.BlockSpec((None, nblk, hb), lambda b, g, i: (bb(b), 0, g)),
                  pl.BlockSpec((None, nblk, hb), lambda b, g, i: (bb(b), 0, g4 + g)),
                  pl.BlockSpec((seq, hb), lambda b, g, i: (bb(b), c_sel + g)),
                  pl.BlockSpec((seq, hb), lambda b, g, i: (bb(b), c_sel + g4 + g)),
                  pl.BlockSpec((seq, hb), lambda b, g, i: (bb(b), c_win + g)),
                  pl.BlockSpec((seq, hb), lambda b, g, i: (bb(b), c_win + g4 + g)),
                  pl.BlockSpec((None, tq, 3 * NSA_GROUP), lambda b, g, i: (g, bb(b) * nq + i, 0))],
        out_specs=pl.BlockSpec((tq, GW), lambda b, g, i: (row(b, i), g)),
        out_shape=jax.ShapeDtypeStruct((mp, QW), BF16),
        compiler_params=_cparams(("parallel", "parallel", "arbitrary")), name="nsa_prompt",
    )(slopes, proj, kcvc, kcvc, proj, proj, proj, proj, gates_t)


def _pool_sample_kernel(pt_ref, *refs, n_page):
    del pt_ref
    c_refs = refs[:n_page]
    wk_ref, wv_ref, ko_ref, vo_ref = refs[n_page:]
    per = c_refs[0].shape[0] // NSA_BLOCK
    for t, c_ref in enumerate(c_refs):
        for half in range(per):
            rows = pl.ds(half * NSA_BLOCK, NSA_BLOCK)
            ko_ref[t * per + half] = jnp.sum(c_ref[rows, 0] * wk_ref[...], axis=0)
            vo_ref[t * per + half] = jnp.sum(c_ref[rows, 1] * wv_ref[...], axis=0)


def _pool_sample(cache, e, page_table, wk, wv):
    nd, n_pages = page_table.shape
    page_rows = cache.shape[2]
    per = page_rows // NSA_BLOCK
    kvh, hd = cache.shape[4], cache.shape[5]
    n_page = math.gcd(n_pages, POOL_PAGES_PER_STEP)
    out = jax.ShapeDtypeStruct((nd, n_pages * per, kvh, hd), F32)
    page_spec = lambda t: pl.BlockSpec((None, None, page_rows, 2, kvh, hd),
                                       lambda b, p, pt: (e, pt[b, p * n_page + t], 0, 0, 0, 0))
    grid_spec = pltpu.PrefetchScalarGridSpec(
        num_scalar_prefetch=1, grid=(nd, n_pages // n_page),
        in_specs=[page_spec(t) for t in range(n_page)]
                 + [pl.BlockSpec((NSA_BLOCK, kvh, hd), lambda b, p, pt: (0, 0, 0))] * 2,
        out_specs=[pl.BlockSpec((None, n_page * per, kvh, hd), lambda b, p, pt: (b, p, 0, 0))] * 2)
    return pl.pallas_call(functools.partial(_pool_sample_kernel, n_page=n_page), grid_spec=grid_spec,
                          out_shape=[out, out],
                          compiler_params=_cparams(("parallel", "parallel")), name="nsa_pool_sample",
                          )(page_table, *([cache] * n_page), wk, wv)


def _head_rows(row, n):
    return jnp.concatenate([row[:, h * HEAD_DIM:(h + 1) * HEAD_DIM] for h in range(n)], axis=0)


def _nsa_cmp_sample_kernel(sl_ref, q_ref, kc_ref, vc_ref, oc_ref, oh_ref, *, past, n_sel):
    b = pl.program_id(0)
    nblk = kc_ref.shape[0]
    scale = HEAD_DIM ** -0.5
    rr = NSA_GROUP
    qh = _head_rows(q_ref[pl.ds(b, 1), :], NSA_HEADS).astype(BF16)
    blk = _iota((1, nblk), 1)
    dist = (past - (blk * NSA_BLOCK + (NSA_BLOCK - 1))).astype(F32)
    last = (past // NSA_BLOCK) - 1
    for g in range(NSA_KV_HEADS):
        qg = jnp.concatenate([qh[g * rr:(g + 1) * rr], jnp.zeros((8 - rr, HEAD_DIM), BF16)], axis=0)
        slope = jnp.concatenate([jnp.full((1, 1), sl_ref[g, r], F32) for r in range(rr)]
                                + [jnp.zeros((8 - rr, 1), F32)], axis=0)
        kc = kc_ref[:, g, :].astype(BF16)
        vc = vc_ref[:, g, :].astype(BF16)
        s = _dot_nt(qg, kc) * scale - slope * dist
        p = _msoftmax(s, blk >= 0)
        oc_ref[g * 8:(g + 1) * 8, :] = jnp.dot(p.astype(BF16), vc, preferred_element_type=F32)
        imp = jnp.sum(p[0:rr], axis=0, keepdims=True)
        imp = jnp.where((blk == 0) | (blk == last), NSA_FORCE, imp)
        impb = jnp.broadcast_to(imp, (nblk, nblk))
        eye = _iota((nblk, nblk), 0) == _iota((nblk, nblk), 1)
        impc = jnp.sum(jnp.where(eye, impb, 0.0), axis=1, keepdims=True)
        before = (impc > impb) | ((impc == impb) & (_iota((nblk, nblk), 0) < _iota((nblk, nblk), 1)))
        rank = jnp.sum(before.astype(F32), axis=0, keepdims=True)
        sel = rank < n_sel
        lower = (_iota((nblk, nblk), 0) < _iota((nblk, nblk), 1)).astype(BF16)
        slot = jnp.dot(jnp.broadcast_to(sel.astype(BF16), (8, nblk)), lower, preferred_element_type=F32)[0:1]
        onehot = sel & (jnp.abs(slot - _iota((NSA_TOPK, nblk), 0).astype(F32)) < 0.5)
        idx = jnp.sum(jnp.where(onehot, _iota((NSA_TOPK, nblk), 1), 0), axis=1, keepdims=True)
        oh_ref[g] = jnp.broadcast_to(idx, (NSA_TOPK, HEAD_DIM))


def _nsa_cmp_sample(proj, kcs, vcs, slopes, row0, past):
    nd, nblk = kcs.shape[0], kcs.shape[1]
    n_sel = min(NSA_TOPK, nblk + 1) - 1
    assert past % NSA_BLOCK == 0 and nblk >= NSA_TOPK
    rb = row0 // SAMPLE_PAD
    return pl.pallas_call(
        functools.partial(_nsa_cmp_sample_kernel, past=past, n_sel=n_sel),
        grid=(nd,),
        in_specs=[pl.BlockSpec(memory_space=pltpu.SMEM),
                  pl.BlockSpec((SAMPLE_PAD, QW), lambda b: (rb, 0)),
                  pl.BlockSpec((None, nblk, NSA_KV_HEADS, HEAD_DIM), lambda b: (b, 0, 0, 0)),
                  pl.BlockSpec((None, nblk, NSA_KV_HEADS, HEAD_DIM), lambda b: (b, 0, 0, 0))],
        out_specs=[pl.BlockSpec((None, 8 * NSA_KV_HEADS, HEAD_DIM), lambda b: (b, 0, 0)),
                   pl.BlockSpec((None, NSA_KV_HEADS, NSA_TOPK, HEAD_DIM), lambda b: (b, 0, 0, 0))],
        out_shape=[jax.ShapeDtypeStruct((nd, 8 * NSA_KV_HEADS, HEAD_DIM), F32),
                   jax.ShapeDtypeStruct((nd, NSA_KV_HEADS, NSA_TOPK, HEAD_DIM), I32)],
        compiler_params=_cparams(("parallel",)), name="nsa_cmp_sample",
    )(slopes, proj, kcs, vcs)


def _nsa_sel_sample_kernel(idx_ref, pt_ref, sl_ref, proj_ref, c0_ref, c1_ref, c2_ref, c3_ref, win_ref, oc_ref,
                           gt_ref, prev_ref, o_ref, m_ref, l_ref, acc_ref, rows_ref, *, past, n_sel, nd):
    del pt_ref, prev_ref
    b = pl.program_id(0)
    j = pl.program_id(1)
    scale = HEAD_DIM ** -0.5
    rr = NSA_GROUP
    c_refs = (c0_ref, c1_ref, c2_ref, c3_ref)

    @pl.when((b == 0) & (j == 0))
    def _():
        rows_ref[...] = jnp.zeros(rows_ref.shape, F32)

    @pl.when(j == 0)
    def _():
        m_ref[...] = jnp.full(m_ref.shape, NEG, F32)
        l_ref[...] = jnp.zeros(l_ref.shape, F32)
        acc_ref[...] = jnp.zeros(acc_ref.shape, F32)

    row = proj_ref[pl.ds(b, 1), :]
    qh = _head_rows(row[:, 0:QW], NSA_HEADS).astype(BF16)

    def group_q(g):
        qg = jnp.concatenate([qh[g * rr:(g + 1) * rr], jnp.zeros((8 - rr, HEAD_DIM), BF16)], axis=0)
        slope = jnp.concatenate([jnp.full((1, 1), sl_ref[g, r], F32) for r in range(rr)]
                                + [jnp.zeros((8 - rr, 1), F32)], axis=0)
        return qg, slope

    groups = range(NSA_KV_HEADS)
    q4 = jnp.stack([group_q(g)[0] for g in groups])
    slope4 = jnp.stack([group_q(g)[1] for g in groups])
    kk = jnp.stack([c_refs[g][:, 0, g, :] for g in groups]).astype(BF16)
    vv = jnp.stack([c_refs[g][:, 1, g, :] for g in groups]).astype(BF16)
    blk4 = jnp.stack([jnp.full((1, 1), idx_ref[(b * NSA_KV_HEADS + g) * NSA_TOPK + j], I32) for g in groups])
    dist = (past - (blk4 * NSA_BLOCK + _iota((1, 1, NSA_BLOCK), 2))).astype(F32)
    sc = _bmm_nt(q4, kk) * scale - slope4 * dist
    m_old = m_ref[:, :, 0:1]
    m_new = jnp.maximum(m_old, jnp.max(sc, axis=-1, keepdims=True))
    alpha = jnp.exp(m_old - m_new)
    e = jnp.exp(sc - m_new)
    l_ref[...] = alpha * l_ref[...] + jnp.sum(e, axis=-1, keepdims=True)
    acc_ref[...] = alpha * acc_ref[...] + _bmm(e.astype(BF16), vv)
    m_ref[...] = jnp.broadcast_to(m_new, m_ref.shape)

    @pl.when(j == n_sel - 1)
    def _():
        gt = _sigmoid(gt_ref[pl.ds(b, 1), :])
        pieces = []
        for g in range(NSA_KV_HEADS):
            qg, slope = group_q(g)
            qf = qg.astype(F32)
            base = QW + g * HEAD_DIM
            k_sel = row[:, base + 2 * KVW: base + 2 * KVW + HEAD_DIM].astype(BF16).astype(F32)
            v_sel = row[:, base + 3 * KVW: base + 3 * KVW + HEAD_DIM].astype(BF16).astype(F32)
            k_win = row[:, base + 4 * KVW: base + 4 * KVW + HEAD_DIM].astype(BF16).astype(F32)
            v_win = row[:, base + 5 * KVW: base + 5 * KVW + HEAD_DIM].astype(BF16).astype(F32)
            rows = slice(g * 8, (g + 1) * 8)
            s_new = jnp.sum(qf * k_sel, axis=-1, keepdims=True) * scale
            m_old = m_ref[g, :, 0:1]
            m_new = jnp.maximum(m_old, s_new)
            alpha = jnp.exp(m_old - m_new)
            e_new = jnp.exp(s_new - m_new)
            l = alpha * l_ref[g, :, 0:1] + e_new
            o_s = (alpha * acc_ref[g] + e_new.astype(BF16).astype(F32) * v_sel) / jnp.maximum(l, 1e-30)
            nwin = win_ref.shape[0]
            kw = win_ref[:, 0, g, :].astype(BF16)
            vw = win_ref[:, 1, g, :].astype(BF16)
            dist = nwin - _iota((1, nwin), 1)
            sw = _dot_nt(qg, kw) * scale - slope * dist.astype(F32)
            valid = dist < NSA_WINDOW
            sw = jnp.where(valid, sw, NEG)
            sw_new = jnp.sum(qf * k_win, axis=-1, keepdims=True) * scale
            mw = jnp.maximum(jnp.max(sw, axis=-1, keepdims=True), sw_new)
            ew = jnp.where(valid, jnp.exp(sw - mw), 0.0)
            ew_new = jnp.exp(sw_new - mw)
            den = jnp.maximum(jnp.sum(ew, axis=-1, keepdims=True) + ew_new, 1e-30)
            o_w = (jnp.dot((ew / den).astype(BF16), vw, preferred_element_type=F32)
                   + (ew_new / den).astype(BF16).astype(F32) * v_win)
            o_c = oc_ref[rows, :]
            for r in range(rr):
                c = g * 3 * rr + 3 * r
                pieces.append(gt[:, c:c + 1] * o_c[r:r + 1] + gt[:, c + 1:c + 2] * o_s[r:r + 1]
                              + gt[:, c + 2:c + 3] * o_w[r:r + 1])
        rows_ref[pl.ds(b, 1), :] = jnp.concatenate(pieces, axis=1)

    @pl.when((b == nd - 1) & (j == n_sel - 1))
    def _():
        o_ref[...] = rows_ref[...].astype(o_ref.dtype)


def _nsa_sel_sample(proj, small, cache, win_cache, e, page_table, idx, o_cmp, slopes, prev, row0, past):
    nd = page_table.shape[0]
    page_rows = cache.shape[2]
    per = page_rows // NSA_BLOCK
    kvh, hd = cache.shape[4], cache.shape[5]
    nwin = win_cache.shape[2]
    n_sel = NSA_TOPK - 1
    rb = row0 // SAMPLE_PAD

    def cache_spec(g):
        def imap(b, j, idx_r, pt_r):
            blk = idx_r[(b * NSA_KV_HEADS + g) * NSA_TOPK + j]
            return (e, pt_r[b, blk // per], blk % per, 1, 0, 0)
        return pl.BlockSpec((None, None, NSA_BLOCK, 2, kvh, hd), imap)

    grid_spec = pltpu.PrefetchScalarGridSpec(
        num_scalar_prefetch=2, grid=(nd, n_sel),
        in_specs=[pl.BlockSpec(memory_space=pltpu.SMEM),
                  pl.BlockSpec((SAMPLE_PAD, proj.shape[1]), lambda b, j, i_r, p_r: (rb, 0)),
                  cache_spec(0), cache_spec(1), cache_spec(2), cache_spec(3),
                  pl.BlockSpec((None, None, nwin, 2, kvh, hd), lambda b, j, i_r, p_r: (e, b, 0, 0, 0, 0)),
                  pl.BlockSpec((None, 8 * NSA_KV_HEADS, HEAD_DIM), lambda b, j, i_r, p_r: (b, 0, 0)),
                  pl.BlockSpec((SAMPLE_PAD, small.shape[1]), lambda b, j, i_r, p_r: (rb, 0)),
                  pl.BlockSpec(memory_space=pl.ANY)],
        out_specs=pl.BlockSpec((SAMPLE_PAD, QW), lambda b, j, i_r, p_r: (rb, 0)),
        scratch_shapes=[pltpu.VMEM((NSA_KV_HEADS, 8, HEAD_DIM), F32)] * 3 + [pltpu.VMEM((SAMPLE_PAD, QW), F32)])
    return pl.pallas_call(
        functools.partial(_nsa_sel_sample_kernel, past=past, n_sel=n_sel, nd=nd),
        grid_spec=grid_spec,
        out_shape=jax.ShapeDtypeStruct(prev.shape, prev.dtype),
        input_output_aliases={11: 0},
        compiler_params=_cparams(("arbitrary", "arbitrary")), name="nsa_sel_sample",
    )(idx, page_table, slopes, proj, cache, cache, cache, cache, win_cache, o_cmp, small, prev)


def _stack_heads(x, n):
    return jnp.stack([x[:, h * HEAD_DIM:(h + 1) * HEAD_DIM] for h in range(n)], axis=0)


def _l2n(x):
    return x * lax.rsqrt(jnp.sum(x * x, axis=-1, keepdims=True) + EPS)


def _softplus(x):
    return jnp.maximum(x, 0.0) + jnp.log(1.0 + jnp.exp(-jnp.abs(x)))


def _bmm(a, b, **kw):
    return jnp.einsum('hcs,hsd->hcd', a, b, preferred_element_type=F32, **kw)


def _bmm_nt(a, b, **kw):
    return jnp.einsum('hcd,hsd->hcs', a, b, preferred_element_type=F32, **kw)


def _bmm_tn(a, b, **kw):
    return jnp.einsum('hcd,hce->hde', a, b, preferred_element_type=F32, **kw)


def _split_bf16(x):
    hi = x.astype(BF16)
    return hi, (x - hi.astype(F32)).astype(BF16)


def _bmm_split(ah, al, bh, bl):
    return _bmm(ah, bh) + (_bmm(ah, bl) + _bmm(al, bh))


def _gdn_prompt_kernel(x_ref, sm_ref, al_ref, dt_ref, z_ref, g_ref, o_ref, s_ref, *, a_col, b_col, batch):
    @pl.when(pl.program_id(0) == batch)
    def _():
        o_ref[...] = jnp.zeros(o_ref.shape, o_ref.dtype)

    @pl.when(pl.program_id(0) < batch)
    def _():
        _gdn_chunk(x_ref, sm_ref, al_ref, dt_ref, z_ref, g_ref, o_ref, s_ref, a_col=a_col, b_col=b_col)


def _gdn_chunk(x_ref, sm_ref, al_ref, dt_ref, z_ref, g_ref, o_ref, s_ref, *, a_col, b_col):
    nh, cc = GDN_HEADS, x_ref.shape[0]
    x = x_ref[...]
    q = _l2n(_stack_heads(x[:, 0:GH], nh)) * (HEAD_DIM ** -0.5)
    k = _l2n(_stack_heads(x[:, GH:2 * GH], nh))
    v = _stack_heads(x[:, 2 * GH:3 * GH], nh)
    sm = sm_ref[...]
    gdec = -jnp.exp(al_ref[...]) * _softplus(sm + dt_ref[...])
    ri, ci = _iota((cc, cc), 0), _iota((cc, cc), 1)
    tri = ri >= ci
    strict = ri > ci
    gcum = jnp.dot(tri.astype(F32), gdec, preferred_element_type=F32, precision=HIGHEST)
    beta = _sigmoid(sm)
    gc = jnp.stack([gcum[:, a_col + h:a_col + h + 1] for h in range(nh)], axis=0)
    bc = jnp.stack([beta[:, b_col + h:b_col + h + 1] for h in range(nh)], axis=0)
    gr = jnp.sum(jnp.where((ri == ci)[None], jnp.broadcast_to(gc, (nh, cc, cc)), 0.0), axis=1, keepdims=True)
    decay = jnp.where(tri[None], jnp.exp(jnp.where(tri[None], gc - gr, 0.0)), 0.0)
    kb = k * bc
    vb = v * bc
    a = jnp.where(strict[None], _bmm_nt(kb, k) * decay, 0.0)
    neg = -a
    tm = jnp.where((ri == ci)[None], 1.0, 0.0) + neg
    pw = neg
    span = 2
    while span < cc:
        ph, plo = _split_bf16(pw)
        pw = _bmm_split(ph, plo, ph, plo)
        ph, plo = _split_bf16(pw)
        th, tlo = _split_bf16(tm)
        tm = tm + _bmm_split(th, tlo, ph, plo)
        span *= 2
    eg = jnp.exp(gc)
    gl = gc[:, cc - 1:cc, :]
    u = _bmm(tm, vb)
    w = _bmm(tm, kb * eg)
    qk = jnp.where(tri[None], _bmm_nt(q, k) * decay, 0.0)
    qe = q * eg
    ke = k * jnp.exp(gl - gc)
    egl = jnp.exp(gl)

    @pl.when(pl.program_id(1) == 0)
    def _():
        s_ref[...] = jnp.zeros(s_ref.shape, F32)

    s = s_ref[...]
    v_new = u - _bmm(w, s)
    o = _bmm(qe, s) + _bmm(qk, v_new)
    s_ref[...] = s * egl + _bmm_tn(ke, v_new)
    z = _stack_heads(z_ref[...], nh)
    o = _rms(o) * g_ref[...][None] * (z * _sigmoid(z))
    for h in range(nh):
        o_ref[:, h * HEAD_DIM:(h + 1) * HEAD_DIM] = o[h].astype(o_ref.dtype)


def _gdn_prompt(qkv_c, small, a_log_row, dt_row, gz, gdn_g, mp, batch, seq, a_col, b_col):
    cc = GDN_CHUNK
    assert cc <= HEAD_DIM
    nch = seq // cc
    bb = lambda b: jnp.minimum(b, batch - 1)
    row = lambda b, n: jnp.where(b < batch, b * nch + n, batch * nch)
    vec = lambda w: pl.BlockSpec((1, w), lambda b, n: (0, 0))
    return pl.pallas_call(
        functools.partial(_gdn_prompt_kernel, a_col=a_col, b_col=b_col, batch=batch),
        grid=(batch + 1, nch),
        in_specs=[pl.BlockSpec((cc, 3 * GH), lambda b, n: (bb(b) * nch + n, 0)),
                  pl.BlockSpec((cc, small.shape[1]), lambda b, n: (bb(b) * nch + n, 0)),
                  vec(small.shape[1]), vec(small.shape[1]),
                  pl.BlockSpec((cc, GH), lambda b, n: (bb(b) * nch + n, 3)),
                  vec(HEAD_DIM)],
        out_specs=[pl.BlockSpec((cc, GH), lambda b, n: (row(b, n), 0)),
                   pl.BlockSpec((None, GDN_HEADS, HEAD_DIM, HEAD_DIM), lambda b, n: (bb(b), 0, 0, 0))],
        out_shape=[jax.ShapeDtypeStruct((mp, GH), BF16),
                   jax.ShapeDtypeStruct((batch, GDN_HEADS, HEAD_DIM, HEAD_DIM), F32)],
        compiler_params=_cparams(("parallel", "arbitrary")), name="gdn_prompt",
    )(qkv_c, small, a_log_row, dt_row, gz, gdn_g.reshape(1, HEAD_DIM))


def _col_of_row(row, n):
    eye = _iota((n, n), 0) == _iota((n, n), 1)
    return jnp.sum(jnp.where(eye, jnp.broadcast_to(row, (n, n)), 0.0), axis=1, keepdims=True)


def _gdn_sample_kernel(x_ref, h_ref, cw_ref, gz_ref, sm_ref, al_ref, dt_ref, g_ref, s0_ref, prev_ref,
                       o_ref, s_ref, rows_ref, *, a_col, b_col, nd):
    del prev_ref
    b = pl.program_id(0)
    nh = GDN_HEADS

    @pl.when(b == 0)
    def _():
        rows_ref[...] = jnp.zeros(rows_ref.shape, F32)

    nk = cw_ref.shape[0]
    conv = jnp.sum(h_ref[...] * cw_ref[0:nk - 1, :], axis=0, keepdims=True) + x_ref[pl.ds(b, 1), :] * cw_ref[nk - 1:nk, :]
    conv = conv * _sigmoid(conv)
    q = _l2n(_head_rows(conv[:, 0:GH], nh)) * (HEAD_DIM ** -0.5)
    k = _l2n(_head_rows(conv[:, GH:2 * GH], nh))
    v = _head_rows(conv[:, 2 * GH:3 * GH], nh)
    sm = sm_ref[pl.ds(b, 1), :]
    gdec = -jnp.exp(al_ref[...]) * _softplus(sm + dt_ref[...])
    eg = jnp.exp(_col_of_row(gdec[:, a_col:a_col + nh], nh))
    beta = _col_of_row(_sigmoid(sm)[:, b_col:b_col + nh], nh)
    s = s0_ref[...]
    pad = lambda t: jnp.concatenate([t[:, None, :], jnp.zeros((nh, 7, HEAD_DIM), F32)], axis=1)
    ks = _bmm(pad(k), s, precision=HIGHEST)[:, 0, :]
    v_new = beta * (v - eg * ks)
    o = eg * _bmm(pad(q), s, precision=HIGHEST)[:, 0, :] + jnp.sum(q * k, axis=-1, keepdims=True) * v_new
    s_ref[...] = s * eg[:, :, None] + _bmm_tn(pad(k), pad(v_new), precision=HIGHEST)
    z = _head_rows(gz_ref[pl.ds(b, 1), :], nh)
    o = _rms(o) * g_ref[...] * (z * _sigmoid(z))
    rows_ref[pl.ds(b, 1), :] = jnp.concatenate([o[h:h + 1] for h in range(nh)], axis=1)

    @pl.when(b == nd - 1)
    def _():
        o_ref[...] = rows_ref[...].astype(o_ref.dtype)


def _gdn_sample(gz, hist, conv_w, small, a_log_row, dt_row, gdn_g, s0, prev, row0, a_col, b_col):
    nd = hist.shape[0]
    rb = row0 // SAMPLE_PAD
    full = lambda shp: pl.BlockSpec(shp, lambda b: (0,) * len(shp))
    return pl.pallas_call(
        functools.partial(_gdn_sample_kernel, a_col=a_col, b_col=b_col, nd=nd),
        grid=(nd,),
        in_specs=[pl.BlockSpec((SAMPLE_PAD, 3 * GH), lambda b: (rb, 0)),
                  pl.BlockSpec((None, GDN_CONV - 1, 3 * GH), lambda b: (b, 0, 0)),
                  full(conv_w.shape),
                  pl.BlockSpec((SAMPLE_PAD, GH), lambda b: (rb, 3)),
                  pl.BlockSpec((SAMPLE_PAD, small.shape[1]), lambda b: (rb, 0)),
                  full(a_log_row.shape), full(dt_row.shape), full((1, HEAD_DIM)),
                  pl.BlockSpec((None, GDN_HEADS, HEAD_DIM, HEAD_DIM), lambda b: (b, 0, 0, 0)),
                  pl.BlockSpec(memory_space=pl.ANY)],
        out_specs=[pl.BlockSpec((SAMPLE_PAD, GH), lambda b: (rb, 0)),
                   pl.BlockSpec((None, GDN_HEADS, HEAD_DIM, HEAD_DIM), lambda b: (b, 0, 0, 0))],
        out_shape=[jax.ShapeDtypeStruct(prev.shape, prev.dtype), jax.ShapeDtypeStruct(s0.shape, F32)],
        scratch_shapes=[pltpu.VMEM((SAMPLE_PAD, GH), F32)],
        input_output_aliases={9: 0},
        compiler_params=_cparams(("arbitrary",)), name="gdn_sample",
    )(gz, hist, conv_w, gz, small, a_log_row, dt_row, gdn_g.reshape(1, HEAD_DIM), s0, prev)


def _even_layer(hn, e, dims, slopes, cache_nsa_kv, cache_nsa_win, state_gdn, state_gdn_conv, page_table,
                w_in, cmp_w, conv_w, a_log, dt_bias, gdn_g, w_out):
    batch, seq, nd, d = dims
    rows_p = batch * seq
    mp = hn.shape[0]
    past = page_table.shape[1] * cache_nsa_kv.shape[2]
    c_nsa = QW + 6 * KVW
    c_gate = c_nsa
    c_gdn = c_gate + 3 * NSA_HEADS
    c_a = c_gdn + 4 * GH
    a_col, b_col = 3 * NSA_HEADS, 3 * NSA_HEADS + GDN_HEADS
    hd = HEAD_DIM

    proj = _matmul([hn], [(w_in, e, d, 0, 0)], (0,), _ep_plain, c_nsa, F32, _col_tile(c_nsa, 512), "proj_nsa")
    tn = _col_tile(4 * GH, 512)
    off = c_gdn - c_nsa
    assert c_nsa % tn == 0 and 0 < off < hd and c_a % hd == off
    gdn_pieces = [(tn, c_nsa // tn, 1, off, tn), (hd, (c_nsa + tn) // hd, tn // hd, 0, off)]
    gz = _matmul([hn], [(w_in, e, d, 0, None, gdn_pieces)], (0,), _ep_plain, 4 * GH, F32, tn, "proj_gdn")
    small_pieces = [(hd, c_nsa // hd, 0, 0, off), (hd, c_a // hd, 0, off, off + 2 * GDN_HEADS)]
    small = _matmul([hn], [(w_in, e, d, 0, None, small_pieces)], (0,), _ep_plain, hd, F32, hd, "proj_small")

    w_exp = jnp.concatenate([jnp.repeat(cmp_w[:, :, 0], hd, axis=1), jnp.repeat(cmp_w[:, :, 1], hd, axis=1)], axis=1)
    kcvc = _pool_prompt(proj, w_exp, batch, seq)
    gates_t = small[:, 0:3 * NSA_HEADS].reshape(mp, NSA_KV_HEADS, 3 * NSA_GROUP).transpose(1, 0, 2)
    o_nsa = _nsa_prompt(proj, kcvc, gates_t, slopes, batch, seq)
    wk = jnp.broadcast_to(cmp_w[:, :, 0][:, :, None], (NSA_BLOCK, NSA_KV_HEADS, hd))
    wv = jnp.broadcast_to(cmp_w[:, :, 1][:, :, None], (NSA_BLOCK, NSA_KV_HEADS, hd))
    kcs, vcs = _pool_sample(cache_nsa_kv, e, page_table, wk, wv)
    o_cmp, sel_rows = _nsa_cmp_sample(proj, kcs, vcs, slopes, rows_p, past)
    idx = sel_rows[:, :, :, 0].reshape(-1)
    o_nsa = _nsa_sel_sample(proj, small, cache_nsa_kv, cache_nsa_win, e, page_table, idx, o_cmp, slopes, o_nsa,
                            rows_p, past)

    pad_row = lambda vec, col: jnp.zeros((1, hd), F32).at[0, col:col + GDN_HEADS].set(vec.astype(F32))
    a_log_row, dt_row = pad_row(a_log, a_col), pad_row(dt_bias, a_col)
    qkv_c = _dwconv_prompt(gz, conv_w, jnp.zeros((3 * GH,), F32), 0, 3 * GH, batch, seq, True, "gdn_conv_prompt")
    o_gdn, s_prompt = _gdn_prompt(qkv_c, small, a_log_row, dt_row, gz, gdn_g, mp, batch, seq, a_col, b_col)
    o_gdn, s_sample = _gdn_sample(gz, state_gdn_conv[e], conv_w, small, a_log_row, dt_row, gdn_g, state_gdn[e], o_gdn,
                                  rows_p, a_col, b_col)

    mixed = _matmul([o_nsa, o_gdn], [(w_out, e, QW, 0, 0), (w_out, e, GH, 1, 0)], (0, 1), _ep_sum, d, F32,
                    _col_tile(d, 512), "proj_out")

    kv_rows = proj[:, QW:QW + 4 * KVW]
    win_rows = proj[:, QW + 4 * KVW:QW + 6 * KVW]
    kv_p = kv_rows[:rows_p].reshape(batch, seq, 4, NSA_KV_HEADS, hd)
    kv_s = kv_rows[rows_p:rows_p + nd].reshape(nd, 1, 4, NSA_KV_HEADS, hd)
    nwin_p = min(NSA_WINDOW, seq)
    win_p = jnp.stack([win_rows[(b + 1) * seq - nwin_p:(b + 1) * seq] for b in range(batch)])
    win_p = win_p.reshape(batch, nwin_p, 2, NSA_KV_HEADS, hd)
    win_all = jnp.concatenate([cache_nsa_win[e], win_rows[rows_p:rows_p + nd].reshape(nd, 1, 2, NSA_KV_HEADS, hd)], axis=1)
    win_s = win_all[:, win_all.shape[1] - min(NSA_WINDOW, win_all.shape[1]):]
    gconv_p = jnp.stack([gz[(b + 1) * seq - (GDN_CONV - 1):(b + 1) * seq, 0:3 * GH] for b in range(batch)])
    gconv_s = jnp.concatenate([state_gdn_conv[e], gz[rows_p:rows_p + nd, 0:3 * GH][:, None]], axis=1)[:, 1:]
    return mixed, (kv_p, kv_s, win_p, win_s, s_prompt, s_sample, gconv_p, gconv_s)


def _odd_layer(hn, o, dims, state_conf_conv, w_pw1, w_dw, b_dw, ln_g, ln_b, w_pw2):
    batch, seq, nd, d = dims
    rows_p = batch * seq
    ch = w_dw.shape[1]
    tn = _col_tile(ch, 256)
    u = _matmul([hn], [(w_pw1, o, d, 0, 0), (w_pw1, o, d, 0, ch // tn)], (0, 0), _ep_glu, ch, F32, tn, "conf_pw1_glu")
    c = _dwconv_prompt(u, w_dw, b_dw, 0, ch, batch, seq, False, "conf_conv_prompt")
    c = _dwconv_sample(u, state_conf_conv[o], w_dw, b_dw, 0, ch, c, rows_p, False, "conf_conv_sample")
    y = _ln_silu(c, ln_g, ln_b)
    mixed = _matmul([y], [(w_pw2, o, ch, 0, 0)], (0,), _ep_plain, d, F32, _col_tile(d, 512), "conf_pw2")
    cc_p = jnp.stack([u[(b + 1) * seq - (CONF_WIDTH - 1):(b + 1) * seq] for b in range(batch)])
    cc_s = jnp.concatenate([state_conf_conv[o], u[rows_p:rows_p + nd][:, None]], axis=1)[:, 1:]
    return mixed, (cc_p, cc_s)


def kernel(x_prompt, x_sample, cache_nsa_kv, cache_nsa_win, state_gdn, state_gdn_conv, state_conf_conv, page_table,
           p_prompt, p_sample, norm_g, w_in_even, cmp_pool_w, gdn_conv_w, gdn_a_log, gdn_dt_bias, gdn_norm_g,
           w_out_even, w_pw1, w_dw, b_dw, conf_ln_g, conf_ln_b, w_pw2, w_ffn_gate, w_ffn_up, w_ffn_down,
           w_ple_gate, w_ple_proj):
    batch, seq, d = x_prompt.shape
    nd = x_sample.shape[0]
    depth = norm_g.shape[0]
    ffn = w_ffn_gate.shape[2]
    ple = w_ple_proj.shape[1]
    rows_p = batch * seq
    assert x_sample.shape[1] == 1 and nd <= 8 and rows_p % SAMPLE_PAD == 0
    dims = (batch, seq, nd, d)
    slopes = jnp.asarray(np.asarray([2.0 ** (-8.0 * (h + 1) / NSA_HEADS) for h in range(NSA_HEADS)], np.float32)
                         .reshape(NSA_KV_HEADS, NSA_GROUP))

    padrows = lambda a: jnp.concatenate([a, jnp.zeros((SAMPLE_PAD - nd,) + a.shape[1:], a.dtype)], axis=0)
    h = jnp.concatenate([x_prompt.reshape(rows_p, d), padrows(x_sample.reshape(nd, d))], axis=0)
    ones = jnp.ones((d,), F32)
    tn_f = _col_tile(ffn, 256)
    w_down_bf16 = w_ffn_down.astype(BF16)
    outs = [[] for _ in range(10)]
    hn = _norm(h, norm_g[0, 0])
    for i in range(depth):
        g_post, f_pre, f_post = norm_g[i, 1], norm_g[i, 2], norm_g[i, 3]
        if i % 2 == 0:
            e = i // 2
            mixed, st = _even_layer(hn, e, dims, slopes, cache_nsa_kv, cache_nsa_win, state_gdn, state_gdn_conv,
                                    page_table, w_in_even, cmp_pool_w[e], gdn_conv_w[e], gdn_a_log[e],
                                    gdn_dt_bias[e], gdn_norm_g[e], w_out_even)
            for lst, val in zip(outs[:8], st):
                lst.append(val)
        else:
            o = i // 2
            mixed, st = _odd_layer(hn, o, dims, state_conf_conv, w_pw1, w_dw[o], b_dw[o], conf_ln_g[o],
                                   conf_ln_b[o], w_pw2)
            outs[8].append(st[0])
            outs[9].append(st[1])
        h, hn = _addnorm(h, mixed, g_post, f_pre)
        act = _matmul([hn], [(w_ffn_gate, i, d, 0, 0), (w_ffn_up, i, d, 0, 0)], (0, 0), _ep_swiglu, ffn, BF16, tn_f,
                      "ffn_gate_up")
        f = _matmul([act], [(w_down_bf16, i, ffn, 0, 0)], (0,), _ep_plain, d, F32, _col_tile(d, 512), "ffn_down",
                    x_stationary=True)
        h, hn = _addnorm(h, f, f_post, ones)
        p_i = jnp.concatenate([p_prompt[i].reshape(rows_p, ple), padrows(p_sample[i].reshape(nd, ple))], axis=0).astype(BF16)
        h = _matmul([hn, p_i], [(w_ple_gate, i, d, 0, 0), (w_ple_proj, i, ple, 0, 0)], (0, 1), _ep_ple, d, F32,
                    _col_tile(d, 512), "ple", res=h)
        if i + 1 < depth:
            hn = _norm(h, norm_g[i + 1, 0])
    y_p = h[:rows_p].reshape(batch, seq, d)
    y_s = h[rows_p:rows_p + nd].reshape(nd, 1, d)
    return (y_p, y_s) + tuple(jnp.stack(lst) for lst in outs)
```

```python
import functools
import math

import numpy as np
import jax
import jax.numpy as jnp
from jax import lax
from jax.experimental import pallas as pl
from jax.experimental.pallas import tpu as pltpu

F32 = jnp.float32
BF16 = jnp.bfloat16
I32 = jnp.int32
HIGHEST = lax.Precision.HIGHEST

HEAD_DIM = 128
NSA_HEADS = 16
NSA_KV_HEADS = 4
NSA_GROUP = NSA_HEADS // NSA_KV_HEADS
NSA_BLOCK = 64
NSA_TOPK = 16
NSA_WINDOW = 512
NSA_FORCE = 1.0e4
GDN_HEADS = 16
GDN_CONV = 4
GDN_CHUNK = 64
CONF_WIDTH = 31
EPS = 1e-6
NEG = -1.0e30

SAMPLE_PAD = 16
V7X_VMEM_LIMIT = 58 * 1024 * 1024
MM_VMEM_BUDGET = 50 * 1024 * 1024
MM_MAX_ROWS = 1024
NSA_QTILE = 256
SEL_KCHUNK = 512
GDN_CHUNKS_PER_STEP = 2
ROW_TILE_ROWS = 320
POOL_PAGES_PER_STEP = 8

QW = NSA_HEADS * HEAD_DIM
GW = NSA_GROUP * HEAD_DIM
KVW = NSA_KV_HEADS * HEAD_DIM
GH = GDN_HEADS * HEAD_DIM


def _cparams(sem):
    return pltpu.CompilerParams(dimension_semantics=sem, vmem_limit_bytes=V7X_VMEM_LIMIT)


def _divisor_tile(n, target, mult):
    best = None
    for t in range(mult, min(n, target) + 1, mult):
        if n % t == 0:
            best = t
    assert best is not None, (n, target, mult)
    return best


def _sigmoid(x):
    return 1.0 / (1.0 + jnp.exp(-x))


def _iota(shape, dim):
    return lax.broadcasted_iota(I32, shape, dim)


def _rms(x):
    return x * lax.rsqrt(jnp.mean(x * x, axis=-1, keepdims=True) + EPS)


def _msoftmax(s, mask):
    s = jnp.where(mask, s, NEG)
    m = jnp.max(s, axis=-1, keepdims=True)
    e = jnp.where(mask, jnp.exp(s - m), 0.0)
    return e / jnp.maximum(jnp.sum(e, axis=-1, keepdims=True), 1e-30)


def _dot_nt(a, b, **kw):
    return lax.dot_general(a, b, (((1,), (1,)), ((), ())), preferred_element_type=F32, **kw)


def _norm_kernel(h_ref, g_ref, o_ref):
    o_ref[...] = (_rms(h_ref[...]) * g_ref[...]).astype(o_ref.dtype)


def _addnorm_kernel(h_ref, f_ref, g1_ref, g2_ref, h2_ref, hn_ref):
    h2 = h_ref[...] + _rms(f_ref[...]) * g1_ref[...]
    h2_ref[...] = h2
    hn_ref[...] = (_rms(h2) * g2_ref[...]).astype(hn_ref.dtype)


def _ln_silu_kernel(c_ref, g_ref, b_ref, o_ref):
    x = c_ref[...]
    xc = x - jnp.mean(x, axis=-1, keepdims=True)
    y = xc * lax.rsqrt(jnp.mean(xc * xc, axis=-1, keepdims=True) + EPS)
    y = y * g_ref[...] + b_ref[...]
    o_ref[...] = (y * _sigmoid(y)).astype(o_ref.dtype)


def _row_call(kernel, mats, vecs, out_dtypes, name):
    mp, d = mats[0].shape
    tr = _divisor_tile(mp, ROW_TILE_ROWS, SAMPLE_PAD)
    mat_spec = pl.BlockSpec((tr, d), lambda i: (i, 0))
    vec_spec = pl.BlockSpec((1, d), lambda i: (0, 0))
    outs = [jax.ShapeDtypeStruct((mp, d), dt) for dt in out_dtypes]
    res = pl.pallas_call(
        kernel, grid=(mp // tr,),
        in_specs=[mat_spec] * len(mats) + [vec_spec] * len(vecs),
        out_specs=[mat_spec] * len(outs), out_shape=outs,
        compiler_params=_cparams(("parallel",)), name=name,
    )(*mats, *[v.reshape(1, d).astype(F32) for v in vecs])
    return res


def _norm(h, g):
    return _row_call(_norm_kernel, [h], [g], [BF16], "rmsnorm")[0]


def _addnorm(h, f, g1, g2):
    return _row_call(_addnorm_kernel, [h, f], [g1, g2], [F32, BF16], "add_rmsnorm")


def _ln_silu(c, g, b):
    return _row_call(_ln_silu_kernel, [c], [g, b], [BF16], "layernorm_silu")[0]


def _ep_plain(accs, res):
    return accs[0]


def _ep_sum(accs, res):
    return accs[0] + accs[1]


def _ep_swiglu(accs, res):
    return accs[0] * _sigmoid(accs[0]) * accs[1]


def _ep_glu(accs, res):
    return accs[0] * _sigmoid(accs[1])


def _ep_ple(accs, res):
    return res + _sigmoid(accs[0]) * accs[1]


def _w_pieces(wspec, tn):
    return wspec[5] if len(wspec) > 5 and wspec[5] is not None else [(tn, wspec[4], 1, 0, tn)]


def _w_transposed(wspec):
    return len(wspec) > 6 and wspec[6]


def _mm_kernel(*refs, x_of_w, pieces, transposed, epilogue, has_res, cast):
    nx = max(x_of_w) + 1
    n_piece = [len(p) for p in pieces]
    x_refs = refs[:nx]
    pos = nx
    w_refs = []
    for n in n_piece:
        w_refs.append(refs[pos:pos + n])
        pos += n
    res_ref = refs[pos] if has_res else None
    pos += int(has_res)
    o_ref = refs[pos]
    wb_refs = refs[pos + 1:] if cast else [r[0] for r in w_refs]

    if cast:
        @pl.when(pl.program_id(1) == 0)
        def _():
            for prefs, pcs, tr, wb_ref in zip(w_refs, pieces, transposed, wb_refs):
                ax = 0 if tr else 1
                cut = (lambda v, c0, c1: v[c0:c1, :]) if tr else (lambda v, c0, c1: v[:, c0:c1])
                parts = [cut(r[...], c0, c1) if (c0, c1) != (0, bw) else r[...] for r, (bw, _, _, c0, c1) in zip(prefs, pcs)]
                fill = wb_ref.shape[ax] - sum(c1 - c0 for (_, _, _, c0, c1) in pcs)
                if fill:
                    shape = (fill, wb_ref.shape[1]) if tr else (wb_ref.shape[0], fill)
                    parts.append(jnp.zeros(shape, F32))
                tile = parts[0] if len(parts) == 1 else jnp.concatenate(parts, axis=ax)
                wb_ref[...] = tile.astype(BF16)

    accs = [(_dot_nt if tr else functools.partial(jnp.dot, preferred_element_type=F32))(x_refs[xi][...], wb[...])
            for xi, wb, tr in zip(x_of_w, wb_refs, transposed)]
    res = res_ref[...] if has_res else None
    o_ref[...] = epilogue(accs, res).astype(o_ref.dtype)


def _mm_row_tile(mp, xs, ws, tn, out_dtype, has_res, x_stationary):
    fixed = 0
    for wspec in ws:
        w, k = wspec[0], wspec[2]
        fixed += 2 * k * sum(p[0] for p in _w_pieces(wspec, tn)) * w.dtype.itemsize
        fixed += k * tn * 2 if w.dtype != BF16 else 0
    per_row = sum(x.shape[1] * x.dtype.itemsize * (1 if x_stationary else 2) for x in xs)
    per_row += 2 * tn * jnp.dtype(out_dtype).itemsize + (2 * tn * 4 if has_res else 0)
    per_row += len(ws) * tn * 4
    best = SAMPLE_PAD
    for tm in range(SAMPLE_PAD, MM_MAX_ROWS + 1, SAMPLE_PAD):
        if mp % tm == 0 and fixed + tm * per_row <= MM_VMEM_BUDGET:
            best = tm
    return best


def _matmul(xs, ws, x_of_w, epilogue, n_out, out_dtype, tn, name, res=None, x_stationary=False):
    mp = xs[0].shape[0]
    assert n_out % tn == 0
    cast = not x_stationary
    assert all((wspec[0].dtype == BF16) == x_stationary for wspec in ws)
    tm = _mm_row_tile(mp, xs, ws, tn, out_dtype, res is not None, x_stationary)
    if x_stationary:
        grid = (mp // tm, n_out // tn)
        ij = lambda a, b: (a, b)
    else:
        grid = (n_out // tn, mp // tm)
        ij = lambda a, b: (b, a)
    x_mode = dict(pipeline_mode=pl.Buffered(1)) if x_stationary else {}
    in_specs = [pl.BlockSpec((tm, x.shape[1]), lambda a, b: (ij(a, b)[0], 0), **x_mode) for x in xs]
    arrs = list(xs)
    scratch = []
    pieces = [_w_pieces(wspec, tn) for wspec in ws]
    transposed = [_w_transposed(wspec) for wspec in ws]
    for wspec, pcs, tr in zip(ws, pieces, transposed):
        w, layer, k, rb = wspec[:4]
        assert (len(pcs) == 1 and not tr) or cast
        for (bw, c0, step, _, _) in pcs:
            col = functools.partial(lambda a, b, c0, step: c0 + step * ij(a, b)[1], c0=c0, step=step)
            if tr:
                in_specs.append(pl.BlockSpec((None, bw, k), functools.partial(
                    lambda a, b, layer, rb, col: (layer, col(a, b), rb), layer=layer, rb=rb, col=col)))
            else:
                in_specs.append(pl.BlockSpec((None, k, bw), functools.partial(
                    lambda a, b, layer, rb, col: (layer, rb, col(a, b)), layer=layer, rb=rb, col=col)))
            arrs.append(w)
        if cast:
            scratch.append(pltpu.VMEM((tn, k) if tr else (k, tn), BF16))
    if res is not None:
        in_specs.append(pl.BlockSpec((tm, tn), lambda a, b: ij(a, b)))
        arrs.append(res)
    return pl.pallas_call(
        functools.partial(_mm_kernel, x_of_w=tuple(x_of_w), pieces=pieces, transposed=transposed, epilogue=epilogue,
                          has_res=res is not None, cast=cast),
        grid=grid,
        in_specs=in_specs,
        out_specs=pl.BlockSpec((tm, tn), lambda a, b: ij(a, b)),
        out_shape=jax.ShapeDtypeStruct((mp, n_out), out_dtype),
        scratch_shapes=scratch,
        compiler_params=_cparams(("arbitrary", "arbitrary")), name=name,
    )(*arrs)


def _col_tile(n, target):
    return _divisor_tile(n, target, 128)


def _dwconv_prompt_kernel(x_ref, w_ref, b_ref, o_ref, xs_ref, wb_ref, *, width, seq, pad, rows, act, batch):
    ct = x_ref.shape[1]

    @pl.when(pl.program_id(0) == batch)
    def _():
        o_ref[...] = jnp.zeros(o_ref.shape, F32)

    @pl.when(pl.program_id(0) < batch)
    def _():
        _dwconv_prompt_body(x_ref, w_ref, b_ref, o_ref, xs_ref, wb_ref, width=width, seq=seq, pad=pad, rows=rows,
                            act=act)


def _dwconv_prompt_body(x_ref, w_ref, b_ref, o_ref, xs_ref, wb_ref, *, width, seq, pad, rows, act):
    ct = x_ref.shape[1]
    xs_ref[0:pad, :] = jnp.zeros((pad, ct), F32)
    xs_ref[pad:pad + seq, :] = x_ref[...]
    bias = b_ref[...]
    for j in range(width):
        wb_ref[j] = jnp.broadcast_to(w_ref[pl.ds(j, 1), :], (8, ct))

    def body(i, carry):
        r0 = pl.multiple_of(i * rows, rows)
        acc = jnp.zeros((rows, ct), F32) + bias
        win = xs_ref[pl.ds(r0, rows + pad), :]
        off = pad - (width - 1)
        for r in range(8):
            taps = [j for j in range(width) if (off + j) % 8 == r]
            if not taps:
                continue
            wr = win if r == 0 else pltpu.roll(win, rows + pad - r, 0)
            for j in taps:
                base = off + j - r
                acc = acc + jnp.concatenate([wb_ref[j]] * (rows // 8), axis=0) * wr[base:base + rows]
        if act:
            acc = acc * _sigmoid(acc)
        o_ref[pl.ds(r0, rows), :] = acc
        return carry

    lax.fori_loop(0, seq // rows, body, 0)


def _dwconv_prompt(x, w, bias, col0, ncols, batch, seq, act, name):
    mp = x.shape[0]
    width = w.shape[0]
    ct = _col_tile(ncols, 512)
    assert col0 % ct == 0
    pad = -(-(width - 1) // 8) * 8
    rows = 32
    return pl.pallas_call(
        functools.partial(_dwconv_prompt_kernel, width=width, seq=seq, pad=pad, rows=rows, act=act, batch=batch),
        grid=(batch + 1, ncols // ct),
        in_specs=[pl.BlockSpec((seq, ct), lambda b, j: (jnp.minimum(b, batch - 1), col0 // ct + j)),
                  pl.BlockSpec((width, ct), lambda b, j: (0, j)),
                  pl.BlockSpec((1, ct), lambda b, j: (0, j))],
        out_specs=pl.BlockSpec((seq, ct), lambda b, j: (b, j)),
        out_shape=jax.ShapeDtypeStruct((mp, ncols), F32),
        scratch_shapes=[pltpu.VMEM((pad + seq, ct), F32), pltpu.VMEM((width, 8, ct), F32)],
        compiler_params=_cparams(("parallel", "parallel")), name=name,
    )(x, w, bias.reshape(1, ncols))


def _dwconv_sample_kernel(x_ref, h_ref, w_ref, b_ref, prev_ref, o_ref, *, width, nb, act):
    del prev_ref
    ct = x_ref.shape[1]
    acc = jnp.sum(h_ref[...] * w_ref[0:width - 1, :][None], axis=1)
    acc = acc + x_ref[0:nb, :] * w_ref[width - 1:width, :] + b_ref[...]
    if act:
        acc = acc * _sigmoid(acc)
    o_ref[0:nb, :] = acc
    o_ref[nb:, :] = jnp.zeros((SAMPLE_PAD - nb, ct), F32)


def _dwconv_sample(x, hist, w, bias, col0, ncols, prev, row0, act, name):
    nb = hist.shape[0]
    width = w.shape[0]
    ct = _col_tile(ncols, 512)
    rb = row0 // SAMPLE_PAD
    return pl.pallas_call(
        functools.partial(_dwconv_sample_kernel, width=width, nb=nb, act=act),
        grid=(ncols // ct,),
        in_specs=[pl.BlockSpec((SAMPLE_PAD, ct), lambda j: (rb, col0 // ct + j)),
                  pl.BlockSpec((nb, width - 1, ct), lambda j: (0, 0, j)),
                  pl.BlockSpec((width, ct), lambda j: (0, j)),
                  pl.BlockSpec((1, ct), lambda j: (0, j)),
                  pl.BlockSpec(memory_space=pl.ANY)],
        out_specs=pl.BlockSpec((SAMPLE_PAD, ct), lambda j: (rb, j)),
        out_shape=jax.ShapeDtypeStruct(prev.shape, F32),
        input_output_aliases={4: 0},
        compiler_params=_cparams(("parallel",)), name=name,
    )(x, hist, w, bias.reshape(1, ncols), prev)


def _pool_prompt_kernel(x_ref, w_ref, o_ref, *, nblk):
    x = x_ref[...].reshape(nblk, NSA_BLOCK, x_ref.shape[1])
    o_ref[...] = jnp.sum(x * w_ref[...][None], axis=1)


def _pool_prompt(proj, w_exp, batch, seq):
    nblk = seq // NSA_BLOCK
    return pl.pallas_call(
        functools.partial(_pool_prompt_kernel, nblk=nblk),
        grid=(batch,),
        in_specs=[pl.BlockSpec((seq, 2 * KVW), lambda b: (b, QW // (2 * KVW))),
                  pl.BlockSpec((NSA_BLOCK, 2 * KVW), lambda b: (0, 0))],
        out_specs=pl.BlockSpec((None, nblk, 2 * KVW), lambda b: (b, 0, 0)),
        out_shape=jax.ShapeDtypeStruct((batch, nblk, 2 * KVW), F32),
        compiler_params=_cparams(("parallel",)), name="nsa_pool_prompt",
    )(proj, w_exp)


def _nsa_prompt_kernel(sl_ref, q_ref, kc_ref, vc_ref, ks_ref, vs_ref, kw_ref, vw_ref, gt_ref, o_ref, *, tq, seq, batch):
    @pl.when(pl.program_id(0) == batch)
    def _():
        o_ref[...] = jnp.zeros(o_ref.shape, o_ref.dtype)

    @pl.when(pl.program_id(0) < batch)
    def _():
        _nsa_prompt_body(sl_ref, q_ref, kc_ref, vc_ref, ks_ref, vs_ref, kw_ref, vw_ref, gt_ref, o_ref, tq=tq, seq=seq)


def _nsa_prompt_body(sl_ref, q_ref, kc_ref, vc_ref, ks_ref, vs_ref, kw_ref, vw_ref, gt_ref, o_ref, *, tq, seq):
    g = pl.program_id(1)
    p0 = pl.program_id(2) * tq
    nblk = seq // NSA_BLOCK
    n_keep = min(NSA_TOPK, nblk)
    scale = HEAD_DIM ** -0.5
    rr = NSA_GROUP
    slopes = [sl_ref[g, r] for r in range(rr)]
    heads = [slice(r * tq, (r + 1) * tq) for r in range(rr)]
    tn_dims = (((0,), (0,)), ((), ()))

    q = q_ref[...]
    qs = jnp.concatenate([q[:, r * HEAD_DIM:(r + 1) * HEAD_DIM] for r in range(rr)], axis=0).astype(BF16)
    pos = p0 + _iota((tq, 1), 0)

    kc = kc_ref[...].astype(BF16)
    vc = vc_ref[...].astype(BF16)
    pos_l = p0 + _iota((1, tq), 1)
    blk = _iota((nblk, 1), 0)
    blk_end = blk * NSA_BLOCK + (NSA_BLOCK - 1)
    valid_c = blk_end <= pos_l
    dist_c = (pos_l - blk_end).astype(F32)
    st = _dot_nt(kc, qs) * scale
    probs = []
    for r in range(rr):
        sr = jnp.where(valid_c, st[:, heads[r]] - slopes[r] * dist_c, NEG)
        e = jnp.where(valid_c, jnp.exp(sr - jnp.max(sr, axis=0, keepdims=True)), 0.0)
        probs.append(e / jnp.maximum(jnp.sum(e, axis=0, keepdims=True), 1e-30))
    pt = jnp.concatenate(probs, axis=1)
    o_c = lax.dot_general(pt.astype(BF16), vc, tn_dims, preferred_element_type=F32)

    imp = probs[0]
    for r in range(1, rr):
        imp = imp + probs[r]
    cur = pos_l // NSA_BLOCK
    forced = (blk == 0) | (blk == cur) | (blk == cur - 1)
    imp = jnp.where(forced, NSA_FORCE, imp)
    imp = jnp.where(blk * NSA_BLOCK > pos_l, -1.0, imp)
    rank = jnp.zeros((nblk, tq), I32)
    for i in range(nblk):
        c = imp[i:i + 1, :]
        rank = rank + ((c > imp) | ((c == imp) & (blk > i))).astype(I32)
    sel_t = (rank < n_keep).astype(BF16)

    kch = min(SEL_KCHUNK, seq)
    n_chunks = (p0 + tq + kch - 1) // kch

    def sel_body(c, carry):
        m, l, acc = carry
        k0 = pl.multiple_of(c * kch, kch)
        kk = ks_ref[pl.ds(k0, kch), :].astype(BF16)
        vv = vs_ref[pl.ds(k0, kch), :].astype(BF16)
        kpos = k0 + _iota((1, kch), 1)
        expand = (_iota((nblk, kch), 0) == ((k0 + _iota((nblk, kch), 1)) // NSA_BLOCK)).astype(BF16)
        selk = lax.dot_general(sel_t, expand, tn_dims, preferred_element_type=F32) > 0.5
        mask = selk & (kpos <= pos)
        kb = (kpos - p0).astype(F32)
        qk = _dot_nt(qs, kk) * scale
        sc = jnp.concatenate([jnp.where(mask, qk[heads[r]] + slopes[r] * kb, NEG) for r in range(rr)], axis=0)
        m_new = jnp.maximum(m, jnp.max(sc, axis=-1, keepdims=True))
        alpha = jnp.exp(m - m_new)
        e = jnp.exp(sc - m_new)
        l = alpha * l + jnp.sum(e, axis=-1, keepdims=True)
        acc = alpha * acc + jnp.dot(e.astype(BF16), vv, preferred_element_type=F32)
        return m_new, l, acc

    init = (jnp.full((rr * tq, 1), NEG, F32), jnp.zeros((rr * tq, 1), F32), jnp.zeros((rr * tq, HEAD_DIM), F32))
    _, l, acc = lax.fori_loop(0, n_chunks, sel_body, init)
    o_s = acc / jnp.maximum(l, 1e-30)

    span = min(NSA_WINDOW + tq, seq)
    w0 = pl.multiple_of(jnp.clip(p0 + tq - span, 0, seq - span), 128)
    kk = kw_ref[pl.ds(w0, span), :].astype(BF16)
    vv = vw_ref[pl.ds(w0, span), :].astype(BF16)
    kpos = w0 + _iota((1, span), 1)
    dist = pos - kpos
    valid_w = (dist >= 0) & (dist < NSA_WINDOW)
    kb = (kpos - p0).astype(F32)
    qk = _dot_nt(qs, kk) * scale
    ews, lws = [], []
    for r in range(rr):
        sr = jnp.where(valid_w, qk[heads[r]] + slopes[r] * kb, NEG)
        e = jnp.exp(sr - jnp.max(sr, axis=-1, keepdims=True))
        ews.append(e)
        lws.append(jnp.sum(e, axis=-1, keepdims=True))
    o_w = jnp.dot(jnp.concatenate(ews, axis=0).astype(BF16), vv, preferred_element_type=F32)

    gt = _sigmoid(gt_ref[...])
    for r in range(rr):
        o_ref[:, r * HEAD_DIM:(r + 1) * HEAD_DIM] = (gt[:, 3 * r:3 * r + 1] * o_c[heads[r]]
                                                    + gt[:, 3 * r + 1:3 * r + 2] * o_s[heads[r]]
                                                    + gt[:, 3 * r + 2:3 * r + 3] * (o_w[heads[r]] / lws[r])
                                                    ).astype(o_ref.dtype)


def _nsa_prompt(proj, kcvc, gates_t, slopes, batch, seq):
    mp = proj.shape[0]
    tq = min(NSA_QTILE, seq)
    nq = seq // tq
    nblk = seq // NSA_BLOCK
    hb = HEAD_DIM
    c_sel = (QW + 2 * KVW) // hb
    c_win = (QW + 4 * KVW) // hb
    g4 = NSA_KV_HEADS
    bb = lambda b: jnp.minimum(b, batch - 1)
    row = lambda b, i: jnp.where(b < batch, b * nq + i, batch * nq)
    return pl.pallas_call(
        functools.partial(_nsa_prompt_kernel, tq=tq, seq=seq, batch=batch),
        grid=(batch + 1, NSA_KV_HEADS, nq),
        in_specs=[pl.BlockSpec(memory_space=pltpu.SMEM),
                  pl.BlockSpec((tq, GW), lambda b, g, i: (bb(b) * nq + i, g)),
                  pl.BlockSpec((None, nblk, hb), lambda b, g, i: (bb(b), 0, g)),
                  pl.BlockSpec((None, nblk, hb), lambda b, g, i: (bb(b), 0, g4 + g)),
                  pl.BlockSpec((seq, hb), lambda b, g, i: (bb(b), c_sel + g)),
                  pl.BlockSpec((seq, hb), lambda b, g, i: (bb(b), c_sel + g4 + g)),
                  pl.BlockSpec((seq, hb), lambda b, g, i: (bb(b), c_win + g)),
                  pl.BlockSpec((seq, hb), lambda b, g, i: (bb(b), c_win + g4 + g)),
                  pl.BlockSpec((None, tq, 3 * NSA_GROUP), lambda b, g, i: (g, bb(b) * nq + i, 0))],
        out_specs=pl.BlockSpec((tq, GW), lambda b, g, i: (row(b, i), g)),
        out_shape=jax.ShapeDtypeStruct((mp, QW), BF16),
        compiler_params=_cparams(("parallel", "parallel", "arbitrary")), name="nsa_prompt",
    )(slopes, proj, kcvc, kcvc, proj, proj, proj, proj, gates_t)


def _pool_sample_kernel(pt_ref, *refs, n_page):
    del pt_ref
    c_refs = refs[:n_page]
    wk_ref, wv_ref, ko_ref, vo_ref = refs[n_page:]
    per = c_refs[0].shape[0] // NSA_BLOCK
    for t, c_ref in enumerate(c_refs):
        for half in range(per):
            rows = pl.ds(half * NSA_BLOCK, NSA_BLOCK)
            ko_ref[t * per + half] = jnp.sum(c_ref[rows, 0] * wk_ref[...], axis=0)
            vo_ref[t * per + half] = jnp.sum(c_ref[rows, 1] * wv_ref[...], axis=0)


def _pool_sample(cache, e, page_table, wk, wv):
    nd, n_pages = page_table.shape
    page_rows = cache.shape[2]
    per = page_rows // NSA_BLOCK
    kvh, hd = cache.shape[4], cache.shape[5]
    n_page = math.gcd(n_pages, POOL_PAGES_PER_STEP)
    out = jax.ShapeDtypeStruct((nd, n_pages * per, kvh, hd), F32)
    page_spec = lambda t: pl.BlockSpec((None, None, page_rows, 2, kvh, hd),
                                       lambda b, p, pt: (e, pt[b, p * n_page + t], 0, 0, 0, 0))
    grid_spec = pltpu.PrefetchScalarGridSpec(
        num_scalar_prefetch=1, grid=(nd, n_pages // n_page),
        in_specs=[page_spec(t) for t in range(n_page)]
                 + [pl.BlockSpec((NSA_BLOCK, kvh, hd), lambda b, p, pt: (0, 0, 0))] * 2,
        out_specs=[pl.BlockSpec((None, n_page * per, kvh, hd), lambda b, p, pt: (b, p, 0, 0))] * 2)
    return pl.pallas_call(functools.partial(_pool_sample_kernel, n_page=n_page), grid_spec=grid_spec,
                          out_shape=[out, out],
                          compiler_params=_cparams(("parallel", "parallel")), name="nsa_pool_sample",
                          )(page_table, *([cache] * n_page), wk, wv)


def _head_rows(row, n):
    return jnp.concatenate([row[:, h * HEAD_DIM:(h + 1) * HEAD_DIM] for h in range(n)], axis=0)


def _nsa_cmp_sample_kernel(sl_ref, q_ref, kc_ref, vc_ref, oc_ref, oh_ref, *, past, n_sel):
    b = pl.program_id(0)
    nblk = kc_ref.shape[0]
    scale = HEAD_DIM ** -0.5
    rr = NSA_GROUP
    qh = _head_rows(q_ref[pl.ds(b, 1), :], NSA_HEADS).astype(BF16)
    blk = _iota((1, nblk), 1)
    dist = (past - (blk * NSA_BLOCK + (NSA_BLOCK - 1))).astype(F32)
    last = (past // NSA_BLOCK) - 1
    for g in range(NSA_KV_HEADS):
        qg = jnp.concatenate([qh[g * rr:(g + 1) * rr], jnp.zeros((8 - rr, HEAD_DIM), BF16)], axis=0)
        slope = jnp.concatenate([jnp.full((1, 1), sl_ref[g, r], F32) for r in range(rr)]
                                + [jnp.zeros((8 - rr, 1), F32)], axis=0)
        kc = kc_ref[:, g, :].astype(BF16)
        vc = vc_ref[:, g, :].astype(BF16)
        s = _dot_nt(qg, kc) * scale - slope * dist
        p = _msoftmax(s, blk >= 0)
        oc_ref[g * 8:(g + 1) * 8, :] = jnp.dot(p.astype(BF16), vc, preferred_element_type=F32)
        imp = jnp.sum(p[0:rr], axis=0, keepdims=True)
        imp = jnp.where((blk == 0) | (blk == last), NSA_FORCE, imp)
        impb = jnp.broadcast_to(imp, (nblk, nblk))
        eye = _iota((nblk, nblk), 0) == _iota((nblk, nblk), 1)
        impc = jnp.sum(jnp.where(eye, impb, 0.0), axis=1, keepdims=True)
        before = (impc > impb) | ((impc == impb) & (_iota((nblk, nblk), 0) < _iota((nblk, nblk), 1)))
        rank = jnp.sum(before.astype(F32), axis=0, keepdims=True)
        sel = rank < n_sel
        lower = (_iota((nblk, nblk), 0) < _iota((nblk, nblk), 1)).astype(BF16)
        slot = jnp.dot(jnp.broadcast_to(sel.astype(BF16), (8, nblk)), lower, preferred_element_type=F32)[0:1]
        onehot = sel & (jnp.abs(slot - _iota((NSA_TOPK, nblk), 0).astype(F32)) < 0.5)
        idx = jnp.sum(jnp.where(onehot, _iota((NSA_TOPK, nblk), 1), 0), axis=1, keepdims=True)
        oh_ref[g] = jnp.broadcast_to(idx, (NSA_TOPK, HEAD_DIM))


def _nsa_cmp_sample(proj, kcs, vcs, slopes, row0, past):
    nd, nblk = kcs.shape[0], kcs.shape[1]
    n_sel = min(NSA_TOPK, nblk + 1) - 1
    assert past % NSA_BLOCK == 0 and nblk >= NSA_TOPK
    rb = row0 // SAMPLE_PAD
    return pl.pallas_call(
        functools.partial(_nsa_cmp_sample_kernel, past=past, n_sel=n_sel),
        grid=(nd,),
        in_specs=[pl.BlockSpec(memory_space=pltpu.SMEM),
                  pl.BlockSpec((SAMPLE_PAD, QW), lambda b: (rb, 0)),
                  pl.BlockSpec((None, nblk, NSA_KV_HEADS, HEAD_DIM), lambda b: (b, 0, 0, 0)),
                  pl.BlockSpec((None, nblk, NSA_KV_HEADS, HEAD_DIM), lambda b: (b, 0, 0, 0))],
        out_specs=[pl.BlockSpec((None, 8 * NSA_KV_HEADS, HEAD_DIM), lambda b: (b, 0, 0)),
                   pl.BlockSpec((None, NSA_KV_HEADS, NSA_TOPK, HEAD_DIM), lambda b: (b, 0, 0, 0))],
        out_shape=[jax.ShapeDtypeStruct((nd, 8 * NSA_KV_HEADS, HEAD_DIM), F32),
                   jax.ShapeDtypeStruct((nd, NSA_KV_HEADS, NSA_TOPK, HEAD_DIM), I32)],
        compiler_params=_cparams(("parallel",)), name="nsa_cmp_sample",
    )(slopes, proj, kcs, vcs)


def _nsa_sel_sample_kernel(idx_ref, pt_ref, sl_ref, proj_ref, c0_ref, c1_ref, c2_ref, c3_ref, win_ref, oc_ref,
                           gt_ref, prev_ref, o_ref, m_ref, l_ref, acc_ref, rows_ref, *, past, n_sel, nd):
    del pt_ref, prev_ref
    b = pl.program_id(0)
    j = pl.program_id(1)
    scale = HEAD_DIM ** -0.5
    rr = NSA_GROUP
    c_refs = (c0_ref, c1_ref, c2_ref, c3_ref)

    @pl.when((b == 0) & (j == 0))
    def _():
        rows_ref[...] = jnp.zeros(rows_ref.shape, F32)

    @pl.when(j == 0)
    def _():
        m_ref[...] = jnp.full(m_ref.shape, NEG, F32)
        l_ref[...] = jnp.zeros(l_ref.shape, F32)
        acc_ref[...] = jnp.zeros(acc_ref.shape, F32)

    row = proj_ref[pl.ds(b, 1), :]
    qh = _head_rows(row[:, 0:QW], NSA_HEADS).astype(BF16)

    def group_q(g):
        qg = jnp.concatenate([qh[g * rr:(g + 1) * rr], jnp.zeros((8 - rr, HEAD_DIM), BF16)], axis=0)
        slope = jnp.concatenate([jnp.full((1, 1), sl_ref[g, r], F32) for r in range(rr)]
                                + [jnp.zeros((8 - rr, 1), F32)], axis=0)
        return qg, slope

    groups = range(NSA_KV_HEADS)
    q4 = jnp.stack([group_q(g)[0] for g in groups])
    slope4 = jnp.stack([group_q(g)[1] for g in groups])
    kk = jnp.stack([c_refs[g][:, 0, g, :] for g in groups]).astype(BF16)
    vv = jnp.stack([c_refs[g][:, 1, g, :] for g in groups]).astype(BF16)
    blk4 = jnp.stack([jnp.full((1, 1), idx_ref[(b * NSA_KV_HEADS + g) * NSA_TOPK + j], I32) for g in groups])
    dist = (past - (blk4 * NSA_BLOCK + _iota((1, 1, NSA_BLOCK), 2))).astype(F32)
    sc = _bmm_nt(q4, kk) * scale - slope4 * dist
    m_old = m_ref[:, :, 0:1]
    m_new = jnp.maximum(m_old, jnp.max(sc, axis=-1, keepdims=True))
    alpha = jnp.exp(m_old - m_new)
    e = jnp.exp(sc - m_new)
    l_ref[...] = alpha * l_ref[...] + jnp.sum(e, axis=-1, keepdims=True)
    acc_ref[...] = alpha * acc_ref[...] + _bmm(e.astype(BF16), vv)
    m_ref[...] = jnp.broadcast_to(m_new, m_ref.shape)

    @pl.when(j == n_sel - 1)
    def _():
        gt = _sigmoid(gt_ref[pl.ds(b, 1), :])
        pieces = []
        for g in range(NSA_KV_HEADS):
            qg, slope = group_q(g)
            qf = qg.astype(F32)
            base = QW + g * HEAD_DIM
            k_sel = row[:, base + 2 * KVW: base + 2 * KVW + HEAD_DIM].astype(BF16).astype(F32)
            v_sel = row[:, base + 3 * KVW: base + 3 * KVW + HEAD_DIM].astype(BF16).astype(F32)
            k_win = row[:, base + 4 * KVW: base + 4 * KVW + HEAD_DIM].astype(BF16).astype(F32)
            v_win = row[:, base + 5 * KVW: base + 5 * KVW + HEAD_DIM].astype(BF16).astype(F32)
            rows = slice(g * 8, (g + 1) * 8)
            s_new = jnp.sum(qf * k_sel, axis=-1, keepdims=True) * scale
            m_old = m_ref[g, :, 0:1]
            m_new = jnp.maximum(m_old, s_new)
            alpha = jnp.exp(m_old - m_new)
            e_new = jnp.exp(s_new - m_new)
            l = alpha * l_ref[g, :, 0:1] + e_new
            o_s = (alpha * acc_ref[g] + e_new.astype(BF16).astype(F32) * v_sel) / jnp.maximum(l, 1e-30)
            nwin = win_ref.shape[0]
            kw = win_ref[:, 0, g, :].astype(BF16)
            vw = win_ref[:, 1, g, :].astype(BF16)
            dist = nwin - _iota((1, nwin), 1)
            sw = _dot_nt(qg, kw) * scale - slope * dist.astype(F32)
            valid = dist < NSA_WINDOW
            sw = jnp.where(valid, sw, NEG)
            sw_new = jnp.sum(qf * k_win, axis=-1, keepdims=True) * scale
            mw = jnp.maximum(jnp.max(sw, axis=-1, keepdims=True), sw_new)
            ew = jnp.where(valid, jnp.exp(sw - mw), 0.0)
            ew_new = jnp.exp(sw_new - mw)
            den = jnp.maximum(jnp.sum(ew, axis=-1, keepdims=True) + ew_new, 1e-30)
            o_w = (jnp.dot((ew / den).astype(BF16), vw, preferred_element_type=F32)
                   + (ew_new / den).astype(BF16).astype(F32) * v_win)
            o_c = oc_ref[rows, :]
            for r in range(rr):
                c = g * 3 * rr + 3 * r
                pieces.append(gt[:, c:c + 1] * o_c[r:r + 1] + gt[:, c + 1:c + 2] * o_s[r:r + 1]
                              + gt[:, c + 2:c + 3] * o_w[r:r + 1])
        rows_ref[pl.ds(b, 1), :] = jnp.concatenate(pieces, axis=1)

    @pl.when((b == nd - 1) & (j == n_sel - 1))
    def _():
        o_ref[...] = rows_ref[...].astype(o_ref.dtype)


def _nsa_sel_sample(proj, small, cache, win_cache, e, page_table, idx, o_cmp, slopes, prev, row0, past):
    nd = page_table.shape[0]
    page_rows = cache.shape[2]
    per = page_rows // NSA_BLOCK
    kvh, hd = cache.shape[4], cache.shape[5]
    nwin = win_cache.shape[2]
    n_sel = NSA_TOPK - 1
    rb = row0 // SAMPLE_PAD

    def cache_spec(g):
        def imap(b, j, idx_r, pt_r):
            blk = idx_r[(b * NSA_KV_HEADS + g) * NSA_TOPK + j]
            return (e, pt_r[b, blk // per], blk % per, 1, 0, 0)
        return pl.BlockSpec((None, None, NSA_BLOCK, 2, kvh, hd), imap)

    grid_spec = pltpu.PrefetchScalarGridSpec(
        num_scalar_prefetch=2, grid=(nd, n_sel),
        in_specs=[pl.BlockSpec(memory_space=pltpu.SMEM),
                  pl.BlockSpec((SAMPLE_PAD, proj.shape[1]), lambda b, j, i_r, p_r: (rb, 0)),
                  cache_spec(0), cache_spec(1), cache_spec(2), cache_spec(3),
                  pl.BlockSpec((None, None, nwin, 2, kvh, hd), lambda b, j, i_r, p_r: (e, b, 0, 0, 0, 0)),
                  pl.BlockSpec((None, 8 * NSA_KV_HEADS, HEAD_DIM), lambda b, j, i_r, p_r: (b, 0, 0)),
                  pl.BlockSpec((SAMPLE_PAD, small.shape[1]), lambda b, j, i_r, p_r: (rb, 0)),
                  pl.BlockSpec(memory_space=pl.ANY)],
        out_specs=pl.BlockSpec((SAMPLE_PAD, QW), lambda b, j, i_r, p_r: (rb, 0)),
        scratch_shapes=[pltpu.VMEM((NSA_KV_HEADS, 8, HEAD_DIM), F32)] * 3 + [pltpu.VMEM((SAMPLE_PAD, QW), F32)])
    return pl.pallas_call(
        functools.partial(_nsa_sel_sample_kernel, past=past, n_sel=n_sel, nd=nd),
        grid_spec=grid_spec,
        out_shape=jax.ShapeDtypeStruct(prev.shape, prev.dtype),
        input_output_aliases={11: 0},
        compiler_params=_cparams(("arbitrary", "arbitrary")), name="nsa_sel_sample",
    )(idx, page_table, slopes, proj, cache, cache, cache, cache, win_cache, o_cmp, small, prev)


def _stack_heads(x, n):
    return jnp.stack([x[:, h * HEAD_DIM:(h + 1) * HEAD_DIM] for h in range(n)], axis=0)


def _l2n(x):
    return x * lax.rsqrt(jnp.sum(x * x, axis=-1, keepdims=True) + EPS)


def _softplus(x):
    return jnp.maximum(x, 0.0) + jnp.log(1.0 + jnp.exp(-jnp.abs(x)))


def _bmm(a, b, **kw):
    return jnp.einsum('hcs,hsd->hcd', a, b, preferred_element_type=F32, **kw)


def _bmm_nt(a, b, **kw):
    return jnp.einsum('hcd,hsd->hcs', a, b, preferred_element_type=F32, **kw)


def _bmm_tn(a, b, **kw):
    return jnp.einsum('hcd,hce->hde', a, b, preferred_element_type=F32, **kw)


def _split_bf16(x):
    hi = x.astype(BF16)
    return hi, (x - hi.astype(F32)).astype(BF16)


def _bmm_split(ah, al, bh, bl):
    return _bmm(ah, bh) + (_bmm(ah, bl) + _bmm(al, bh))


def _gdn_prompt_kernel(x_ref, sm_ref, al_ref, dt_ref, z_ref, g_ref, o_ref, s_ref, *, a_col, b_col, batch):
    @pl.when(pl.program_id(0) == batch)
    def _():
        o_ref[...] = jnp.zeros(o_ref.shape, o_ref.dtype)

    @pl.when(pl.program_id(0) < batch)
    def _():
        _gdn_chunk(x_ref, sm_ref, al_ref, dt_ref, z_ref, g_ref, o_ref, s_ref, a_col=a_col, b_col=b_col)


def _gdn_chunk(x_ref, sm_ref, al_ref, dt_ref, z_ref, g_ref, o_ref, s_ref, *, a_col, b_col):
    nh, cc = GDN_HEADS, GDN_CHUNK
    nc = x_ref.shape[0] // cc
    nb = nc * nh
    rows = [slice(c * cc, (c + 1) * cc) for c in range(nc)]
    x = x_ref[...]
    stack = lambda c0: jnp.concatenate([_stack_heads(x[r, c0:c0 + GH], nh) for r in rows], axis=0)
    q = _l2n(stack(0)) * (HEAD_DIM ** -0.5)
    k = _l2n(stack(GH))
    v = stack(2 * GH)
    sm = sm_ref[...]
    gdec = -jnp.exp(al_ref[...]) * _softplus(sm + dt_ref[...])
    ri, ci = _iota((cc, cc), 0), _iota((cc, cc), 1)
    tri = ri >= ci
    strict = ri > ci
    gcum = [jnp.dot(tri.astype(F32), gdec[r], preferred_element_type=F32, precision=HIGHEST) for r in rows]
    beta = _sigmoid(sm)
    gc = jnp.stack([gcum[c][:, a_col + h:a_col + h + 1] for c in range(nc) for h in range(nh)], axis=0)
    bc = jnp.stack([beta[r][:, b_col + h:b_col + h + 1] for r in rows for h in range(nh)], axis=0)
    gr = jnp.sum(jnp.where((ri == ci)[None], jnp.broadcast_to(gc, (nb, cc, cc)), 0.0), axis=1, keepdims=True)
    decay = jnp.where(tri[None], jnp.exp(jnp.where(tri[None], gc - gr, 0.0)), 0.0)
    kb = k * bc
    vb = v * bc
    a = jnp.where(strict[None], _bmm_nt(kb, k) * decay, 0.0)
    neg = -a
    tm = jnp.where((ri == ci)[None], 1.0, 0.0) + neg
    pw = neg
    span = 2
    while span < cc:
        ph, plo = _split_bf16(pw)
        pw = _bmm_split(ph, plo, ph, plo)
        ph, plo = _split_bf16(pw)
        th, tlo = _split_bf16(tm)
        tm = tm + _bmm_split(th, tlo, ph, plo)
        span *= 2
    eg = jnp.exp(gc)
    gl = gc[:, cc - 1:cc, :]
    u = _bmm(tm, vb)
    w = _bmm(tm, kb * eg)
    qk = jnp.where(tri[None], _bmm_nt(q, k) * decay, 0.0)
    qe = q * eg
    ke = k * jnp.exp(gl - gc)
    egl = jnp.exp(gl)

    @pl.when(pl.program_id(1) == 0)
    def _():
        s_ref[...] = jnp.zeros(s_ref.shape, F32)

    s = s_ref[...]
    for c in range(nc):
        hs = slice(c * nh, (c + 1) * nh)
        v_new = u[hs] - _bmm(w[hs], s)
        o = _bmm(qe[hs], s) + _bmm(qk[hs], v_new)
        s = s * egl[hs] + _bmm_tn(ke[hs], v_new)
        z = _stack_heads(z_ref[rows[c], :], nh)
        o = _rms(o) * g_ref[...][None] * (z * _sigmoid(z))
        for h in range(nh):
            o_ref[rows[c], h * HEAD_DIM:(h + 1) * HEAD_DIM] = o[h].astype(o_ref.dtype)
    s_ref[...] = s


def _gdn_prompt(qkv_c, small, a_log_row, dt_row, gz, gdn_g, mp, batch, seq, a_col, b_col):
    assert GDN_CHUNK <= HEAD_DIM
    cc = GDN_CHUNK * math.gcd(seq // GDN_CHUNK, GDN_CHUNKS_PER_STEP)
    nch = seq // cc
    bb = lambda b: jnp.minimum(b, batch - 1)
    row = lambda b, n: jnp.where(b < batch, b * nch + n, batch * nch)
    vec = lambda w: pl.BlockSpec((1, w), lambda b, n: (0, 0))
    return pl.pallas_call(
        functools.partial(_gdn_prompt_kernel, a_col=a_col, b_col=b_col, batch=batch),
        grid=(batch + 1, nch),
        in_specs=[pl.BlockSpec((cc, 3 * GH), lambda b, n: (bb(b) * nch + n, 0)),
                  pl.BlockSpec((cc, small.shape[1]), lambda b, n: (bb(b) * nch + n, 0)),
                  vec(small.shape[1]), vec(small.shape[1]),
                  pl.BlockSpec((cc, GH), lambda b, n: (bb(b) * nch + n, 3)),
                  vec(HEAD_DIM)],
        out_specs=[pl.BlockSpec((cc, GH), lambda b, n: (row(b, n), 0)),
                   pl.BlockSpec((None, GDN_HEADS, HEAD_DIM, HEAD_DIM), lambda b, n: (bb(b), 0, 0, 0))],
        out_shape=[jax.ShapeDtypeStruct((mp, GH), BF16),
                   jax.ShapeDtypeStruct((batch, GDN_HEADS, HEAD_DIM, HEAD_DIM), F32)],
        compiler_params=_cparams(("parallel", "arbitrary")), name="gdn_prompt",
    )(qkv_c, small, a_log_row, dt_row, gz, gdn_g.reshape(1, HEAD_DIM))


def _col_of_row(row, n):
    eye = _iota((n, n), 0) == _iota((n, n), 1)
    return jnp.sum(jnp.where(eye, jnp.broadcast_to(row, (n, n)), 0.0), axis=1, keepdims=True)


def _gdn_sample_kernel(x_ref, h_ref, cw_ref, gz_ref, sm_ref, al_ref, dt_ref, g_ref, s0_ref, prev_ref,
                       o_ref, s_ref, rows_ref, *, a_col, b_col, nd):
    del prev_ref
    b = pl.program_id(0)
    nh = GDN_HEADS

    @pl.when(b == 0)
    def _():
        rows_ref[...] = jnp.zeros(rows_ref.shape, F32)

    nk = cw_ref.shape[0]
    conv = jnp.sum(h_ref[...] * cw_ref[0:nk - 1, :], axis=0, keepdims=True) + x_ref[pl.ds(b, 1), :] * cw_ref[nk - 1:nk, :]
    conv = conv * _sigmoid(conv)
    q = _l2n(_head_rows(conv[:, 0:GH], nh)) * (HEAD_DIM ** -0.5)
    k = _l2n(_head_rows(conv[:, GH:2 * GH], nh))
    v = _head_rows(conv[:, 2 * GH:3 * GH], nh)
    sm = sm_ref[pl.ds(b, 1), :]
    gdec = -jnp.exp(al_ref[...]) * _softplus(sm + dt_ref[...])
    eg = jnp.exp(_col_of_row(gdec[:, a_col:a_col + nh], nh))
    beta = _col_of_row(_sigmoid(sm)[:, b_col:b_col + nh], nh)
    s = s0_ref[...]
    pad = lambda t: jnp.concatenate([t[:, None, :], jnp.zeros((nh, 7, HEAD_DIM), F32)], axis=1)
    ks = _bmm(pad(k), s, precision=HIGHEST)[:, 0, :]
    v_new = beta * (v - eg * ks)
    o = eg * _bmm(pad(q), s, precision=HIGHEST)[:, 0, :] + jnp.sum(q * k, axis=-1, keepdims=True) * v_new
    s_ref[...] = s * eg[:, :, None] + _bmm_tn(pad(k), pad(v_new), precision=HIGHEST)
    z = _head_rows(gz_ref[pl.ds(b, 1), :], nh)
    o = _rms(o) * g_ref[...] * (z * _sigmoid(z))
    rows_ref[pl.ds(b, 1), :] = jnp.concatenate([o[h:h + 1] for h in range(nh)], axis=1)

    @pl.when(b == nd - 1)
    def _():
        o_ref[...] = rows_ref[...].astype(o_ref.dtype)


def _gdn_sample(gz, hist, conv_w, small, a_log_row, dt_row, gdn_g, s0, prev, row0, a_col, b_col):
    nd = hist.shape[0]
    rb = row0 // SAMPLE_PAD
    full = lambda shp: pl.BlockSpec(shp, lambda b: (0,) * len(shp))
    return pl.pallas_call(
        functools.partial(_gdn_sample_kernel, a_col=a_col, b_col=b_col, nd=nd),
        grid=(nd,),
        in_specs=[pl.BlockSpec((SAMPLE_PAD, 3 * GH), lambda b: (rb, 0)),
                  pl.BlockSpec((None, GDN_CONV - 1, 3 * GH), lambda b: (b, 0, 0)),
                  full(conv_w.shape),
                  pl.BlockSpec((SAMPLE_PAD, GH), lambda b: (rb, 3)),
                  pl.BlockSpec((SAMPLE_PAD, small.shape[1]), lambda b: (rb, 0)),
                  full(a_log_row.shape), full(dt_row.shape), full((1, HEAD_DIM)),
                  pl.BlockSpec((None, GDN_HEADS, HEAD_DIM, HEAD_DIM), lambda b: (b, 0, 0, 0)),
                  pl.BlockSpec(memory_space=pl.ANY)],
        out_specs=[pl.BlockSpec((SAMPLE_PAD, GH), lambda b: (rb, 0)),
                   pl.BlockSpec((None, GDN_HEADS, HEAD_DIM, HEAD_DIM), lambda b: (b, 0, 0, 0))],
        out_shape=[jax.ShapeDtypeStruct(prev.shape, prev.dtype), jax.ShapeDtypeStruct(s0.shape, F32)],
        scratch_shapes=[pltpu.VMEM((SAMPLE_PAD, GH), F32)],
        input_output_aliases={9: 0},
        compiler_params=_cparams(("arbitrary",)), name="gdn_sample",
    )(gz, hist, conv_w, gz, small, a_log_row, dt_row, gdn_g.reshape(1, HEAD_DIM), s0, prev)


def _even_layer(hn, e, dims, slopes, cache_nsa_kv, cache_nsa_win, state_gdn, state_gdn_conv, page_table,
                w_in, cmp_w, conv_w, a_log, dt_bias, gdn_g, w_out):
    batch, seq, nd, d = dims
    rows_p = batch * seq
    mp = hn.shape[0]
    past = page_table.shape[1] * cache_nsa_kv.shape[2]
    c_nsa = QW + 6 * KVW
    c_gate = c_nsa
    c_gdn = c_gate + 3 * NSA_HEADS
    c_a = c_gdn + 4 * GH
    a_col, b_col = 3 * NSA_HEADS, 3 * NSA_HEADS + GDN_HEADS
    hd = HEAD_DIM

    proj = _matmul([hn], [(w_in, e, d, 0, 0, None, True)], (0,), _ep_plain, c_nsa, F32, _col_tile(c_nsa, 512), "proj_nsa")
    tn = _col_tile(4 * GH, 512)
    off = c_gdn - c_nsa
    assert c_nsa % tn == 0 and 0 < off < hd and off % 8 == 0 and c_a % hd == off
    gdn_pieces = [(tn, c_nsa // tn, 1, off, tn), (hd, (c_nsa + tn) // hd, tn // hd, 0, off)]
    gz = _matmul([hn], [(w_in, e, d, 0, None, gdn_pieces, True)], (0,), _ep_plain, 4 * GH, F32, tn, "proj_gdn")
    small_pieces = [(hd, c_nsa // hd, 0, 0, off), (hd, c_a // hd, 0, off, off + 2 * GDN_HEADS)]
    small = _matmul([hn], [(w_in, e, d, 0, None, small_pieces, True)], (0,), _ep_plain, hd, F32, hd, "proj_small")

    w_exp = jnp.concatenate([jnp.repeat(cmp_w[:, :, 0], hd, axis=1), jnp.repeat(cmp_w[:, :, 1], hd, axis=1)], axis=1)
    kcvc = _pool_prompt(proj, w_exp, batch, seq)
    gates_t = small[:, 0:3 * NSA_HEADS].reshape(mp, NSA_KV_HEADS, 3 * NSA_GROUP).transpose(1, 0, 2)
    o_nsa = _nsa_prompt(proj, kcvc, gates_t, slopes, batch, seq)
    wk = jnp.broadcast_to(cmp_w[:, :, 0][:, :, None], (NSA_BLOCK, NSA_KV_HEADS, hd))
    wv = jnp.broadcast_to(cmp_w[:, :, 1][:, :, None], (NSA_BLOCK, NSA_KV_HEADS, hd))
    kcs, vcs = _pool_sample(cache_nsa_kv, e, page_table, wk, wv)
    o_cmp, sel_rows = _nsa_cmp_sample(proj, kcs, vcs, slopes, rows_p, past)
    idx = sel_rows[:, :, :, 0].reshape(-1)
    o_nsa = _nsa_sel_sample(proj, small, cache_nsa_kv, cache_nsa_win, e, page_table, idx, o_cmp, slopes, o_nsa,
                            rows_p, past)

    pad_row = lambda vec, col: jnp.zeros((1, hd), F32).at[0, col:col + GDN_HEADS].set(vec.astype(F32))
    a_log_row, dt_row = pad_row(a_log, a_col), pad_row(dt_bias, a_col)
    qkv_c = _dwconv_prompt(gz, conv_w, jnp.zeros((3 * GH,), F32), 0, 3 * GH, batch, seq, True, "gdn_conv_prompt")
    o_gdn, s_prompt = _gdn_prompt(qkv_c, small, a_log_row, dt_row, gz, gdn_g, mp, batch, seq, a_col, b_col)
    o_gdn, s_sample = _gdn_sample(gz, state_gdn_conv[e], conv_w, small, a_log_row, dt_row, gdn_g, state_gdn[e], o_gdn,
                                  rows_p, a_col, b_col)

    mixed = _matmul([o_nsa, o_gdn], [(w_out, e, QW, 0, 0), (w_out, e, GH, 1, 0)], (0, 1), _ep_sum, d, F32,
                    _col_tile(d, 512), "proj_out")

    kv_rows = proj[:, QW:QW + 4 * KVW]
    win_rows = proj[:, QW + 4 * KVW:QW + 6 * KVW]
    kv_p = kv_rows[:rows_p].reshape(batch, seq, 4, NSA_KV_HEADS, hd)
    kv_s = kv_rows[rows_p:rows_p + nd].reshape(nd, 1, 4, NSA_KV_HEADS, hd)
    nwin_p = min(NSA_WINDOW, seq)
    win_p = jnp.stack([win_rows[(b + 1) * seq - nwin_p:(b + 1) * seq] for b in range(batch)])
    win_p = win_p.reshape(batch, nwin_p, 2, NSA_KV_HEADS, hd)
    win_all = jnp.concatenate([cache_nsa_win[e], win_rows[rows_p:rows_p + nd].reshape(nd, 1, 2, NSA_KV_HEADS, hd)], axis=1)
    win_s = win_all[:, win_all.shape[1] - min(NSA_WINDOW, win_all.shape[1]):]
    gconv_p = jnp.stack([gz[(b + 1) * seq - (GDN_CONV - 1):(b + 1) * seq, 0:3 * GH] for b in range(batch)])
    gconv_s = jnp.concatenate([state_gdn_conv[e], gz[rows_p:rows_p + nd, 0:3 * GH][:, None]], axis=1)[:, 1:]
    return mixed, (kv_p, kv_s, win_p, win_s, s_prompt, s_sample, gconv_p, gconv_s)


def _odd_layer(hn, o, dims, state_conf_conv, w_pw1, w_dw, b_dw, ln_g, ln_b, w_pw2):
    batch, seq, nd, d = dims
    rows_p = batch * seq
    ch = w_dw.shape[1]
    tn = _col_tile(ch, 256)
    u = _matmul([hn], [(w_pw1, o, d, 0, 0), (w_pw1, o, d, 0, ch // tn)], (0, 0), _ep_glu, ch, F32, tn, "conf_pw1_glu")
    c = _dwconv_prompt(u, w_dw, b_dw, 0, ch, batch, seq, False, "conf_conv_prompt")
    c = _dwconv_sample(u, state_conf_conv[o], w_dw, b_dw, 0, ch, c, rows_p, False, "conf_conv_sample")
    y = _ln_silu(c, ln_g, ln_b)
    mixed = _matmul([y], [(w_pw2, o, ch, 0, 0)], (0,), _ep_plain, d, F32, _col_tile(d, 512), "conf_pw2")
    cc_p = jnp.stack([u[(b + 1) * seq - (CONF_WIDTH - 1):(b + 1) * seq] for b in range(batch)])
    cc_s = jnp.concatenate([state_conf_conv[o], u[rows_p:rows_p + nd][:, None]], axis=1)[:, 1:]
    return mixed, (cc_p, cc_s)


def kernel(x_prompt, x_sample, cache_nsa_kv, cache_nsa_win, state_gdn, state_gdn_conv, state_conf_conv, page_table,
           p_prompt, p_sample, norm_g, w_in_even, cmp_pool_w, gdn_conv_w, gdn_a_log, gdn_dt_bias, gdn_norm_g,
           w_out_even, w_pw1, w_dw, b_dw, conf_ln_g, conf_ln_b, w_pw2, w_ffn_gate, w_ffn_up, w_ffn_down,
           w_ple_gate, w_ple_proj):
    batch, seq, d = x_prompt.shape
    nd = x_sample.shape[0]
    depth = norm_g.shape[0]
    ffn = w_ffn_gate.shape[2]
    ple = w_ple_proj.shape[1]
    rows_p = batch * seq
    assert x_sample.shape[1] == 1 and nd <= 8 and rows_p % SAMPLE_PAD == 0
    dims = (batch, seq, nd, d)
    slopes = jnp.asarray(np.asarray([2.0 ** (-8.0 * (h + 1) / NSA_HEADS) for h in range(NSA_HEADS)], np.float32)
                         .reshape(NSA_KV_HEADS, NSA_GROUP))

    padrows = lambda a: jnp.concatenate([a, jnp.zeros((SAMPLE_PAD - nd,) + a.shape[1:], a.dtype)], axis=0)
    h = jnp.concatenate([x_prompt.reshape(rows_p, d), padrows(x_sample.reshape(nd, d))], axis=0)
    ones = jnp.ones((d,), F32)
    tn_f = _col_tile(ffn, 256)
    w_down_bf16 = w_ffn_down.astype(BF16)
    w_in_t = jnp.swapaxes(w_in_even, 1, 2)
    outs = [[] for _ in range(10)]
    hn = _norm(h, norm_g[0, 0])
    for i in range(depth):
        g_post, f_pre, f_post = norm_g[i, 1], norm_g[i, 2], norm_g[i, 3]
        if i % 2 == 0:
            e = i // 2
            mixed, st = _even_layer(hn, e, dims, slopes, cache_nsa_kv, cache_nsa_win, state_gdn, state_gdn_conv,
                                    page_table, w_in_t, cmp_pool_w[e], gdn_conv_w[e], gdn_a_log[e],
                                    gdn_dt_bias[e], gdn_norm_g[e], w_out_even)
            for lst, val in zip(outs[:8], st):
                lst.append(val)
        else:
            o = i // 2
            mixed, st = _odd_layer(hn, o, dims, state_conf_conv, w_pw1, w_dw[o], b_dw[o], conf_ln_g[o],
                                   conf_ln_b[o], w_pw2)
            outs[8].append(st[0])
            outs[9].append(st[1])
        h, hn = _addnorm(h, mixed, g_post, f_pre)
        act = _matmul([hn], [(w_ffn_gate, i, d, 0, 0), (w_ffn_up, i, d, 0, 0)], (0, 0), _ep_swiglu, ffn, BF16, tn_f,
                      "ffn_gate_up")
        f = _matmul([act], [(w_down_bf16, i, ffn, 0, 0)], (0,), _ep_plain, d, F32, _col_tile(d, 512), "ffn_down",
                    x_stationary=True)
        h, hn = _addnorm(h, f, f_post, ones)
        p_i = jnp.concatenate([p_prompt[i].reshape(rows_p, ple), padrows(p_sample[i].reshape(nd, ple))], axis=0).astype(BF16)
        h = _matmul([hn, p_i], [(w_ple_gate, i, d, 0, 0), (w_ple_proj, i, ple, 0, 0)], (0, 1), _ep_ple, d, F32,
                    _col_tile(d, 512), "ple", res=h)
        if i + 1 < depth:
            hn = _norm(h, norm_g[i + 1, 0])
    y_p = h[:rows_p].reshape(batch, seq, d)
    y_s = h[rows_p:rows_p + nd].reshape(nd, 1, d)
    return (y_p, y_s) + tuple(jnp.stack(lst) for lst in outs)
```

```python
import functools
import math

import numpy as np
import jax
import jax.numpy as jnp
from jax import lax
from jax.experimental import pallas as pl
from jax.experimental.pallas import tpu as pltpu

F32 = jnp.float32
BF16 = jnp.bfloat16
I32 = jnp.int32
HIGHEST = lax.Precision.HIGHEST

HEAD_DIM = 128
NSA_HEADS = 16
NSA_KV_HEADS = 4
NSA_GROUP = NSA_HEADS // NSA_KV_HEADS
NSA_BLOCK = 64
NSA_TOPK = 16
NSA_WINDOW = 512
NSA_FORCE = 1.0e4
GDN_HEADS = 16
GDN_CONV = 4
GDN_CHUNK = 64
CONF_WIDTH = 31
EPS = 1e-6
NEG = -1.0e30

SAMPLE_PAD = 16
V7X_VMEM_LIMIT = 58 * 1024 * 1024
MM_VMEM_BUDGET = 50 * 1024 * 1024
MM_MAX_ROWS = 1024
NSA_QTILE = 256
SEL_KCHUNK = 512
GDN_CHUNKS_PER_STEP = 2
ROW_TILE_ROWS = 320
POOL_PAGES_PER_STEP = 8

QW = NSA_HEADS * HEAD_DIM
GW = NSA_GROUP * HEAD_DIM
KVW = NSA_KV_HEADS * HEAD_DIM
GH = GDN_HEADS * HEAD_DIM


def _cparams(sem):
    return pltpu.CompilerParams(dimension_semantics=sem, vmem_limit_bytes=V7X_VMEM_LIMIT)


def _divisor_tile(n, target, mult):
    best = None
    for t in range(mult, min(n, target) + 1, mult):
        if n % t == 0:
            best = t
    assert best is not None, (n, target, mult)
    return best


def _sigmoid(x):
    return 1.0 / (1.0 + jnp.exp(-x))


def _iota(shape, dim):
    return lax.broadcasted_iota(I32, shape, dim)


def _rms(x):
    return x * lax.rsqrt(jnp.mean(x * x, axis=-1, keepdims=True) + EPS)


def _msoftmax(s, mask):
    s = jnp.where(mask, s, NEG)
    m = jnp.max(s, axis=-1, keepdims=True)
    e = jnp.where(mask, jnp.exp(s - m), 0.0)
    return e / jnp.maximum(jnp.sum(e, axis=-1, keepdims=True), 1e-30)


def _dot_nt(a, b, **kw):
    return lax.dot_general(a, b, (((1,), (1,)), ((), ())), preferred_element_type=F32, **kw)


def _norm_kernel(h_ref, g_ref, o_ref):
    o_ref[...] = (_rms(h_ref[...]) * g_ref[...]).astype(o_ref.dtype)


def _addnorm_kernel(h_ref, f_ref, g1_ref, g2_ref, h2_ref, hn_ref):
    h2 = h_ref[...] + _rms(f_ref[...]) * g1_ref[...]
    h2_ref[...] = h2
    hn_ref[...] = (_rms(h2) * g2_ref[...]).astype(hn_ref.dtype)


def _ln_silu_kernel(c_ref, g_ref, b_ref, o_ref):
    x = c_ref[...]
    xc = x - jnp.mean(x, axis=-1, keepdims=True)
    y = xc * lax.rsqrt(jnp.mean(xc * xc, axis=-1, keepdims=True) + EPS)
    y = y * g_ref[...] + b_ref[...]
    o_ref[...] = (y * _sigmoid(y)).astype(o_ref.dtype)


def _row_call(kernel, mats, vecs, out_dtypes, name):
    mp, d = mats[0].shape
    tr = _divisor_tile(mp, ROW_TILE_ROWS, SAMPLE_PAD)
    mat_spec = pl.BlockSpec((tr, d), lambda i: (i, 0))
    vec_spec = pl.BlockSpec((1, d), lambda i: (0, 0))
    outs = [jax.ShapeDtypeStruct((mp, d), dt) for dt in out_dtypes]
    res = pl.pallas_call(
        kernel, grid=(mp // tr,),
        in_specs=[mat_spec] * len(mats) + [vec_spec] * len(vecs),
        out_specs=[mat_spec] * len(outs), out_shape=outs,
        compiler_params=_cparams(("parallel",)), name=name,
    )(*mats, *[v.reshape(1, d).astype(F32) for v in vecs])
    return res


def _norm(h, g):
    return _row_call(_norm_kernel, [h], [g], [BF16], "rmsnorm")[0]


def _addnorm(h, f, g1, g2):
    return _row_call(_addnorm_kernel, [h, f], [g1, g2], [F32, BF16], "add_rmsnorm")


def _ln_silu(c, g, b):
    return _row_call(_ln_silu_kernel, [c], [g, b], [BF16], "layernorm_silu")[0]


def _ep_plain(accs, res):
    return accs[0]


def _ep_sum(accs, res):
    return accs[0] + accs[1]


def _ep_swiglu(accs, res):
    return accs[0] * _sigmoid(accs[0]) * accs[1]


def _ep_glu(accs, res):
    return accs[0] * _sigmoid(accs[1])


def _ep_ple(accs, res):
    return res + _sigmoid(accs[0]) * accs[1]


def _w_pieces(wspec, tn):
    return wspec[5] if len(wspec) > 5 and wspec[5] is not None else [(tn, wspec[4], 1, 0, tn)]


def _w_transposed(wspec):
    return len(wspec) > 6 and wspec[6]


def _mm_kernel(*refs, x_of_w, pieces, transposed, epilogue, has_res, cast):
    nx = max(x_of_w) + 1
    n_piece = [len(p) for p in pieces]
    x_refs = refs[:nx]
    pos = nx
    w_refs = []
    for n in n_piece:
        w_refs.append(refs[pos:pos + n])
        pos += n
    res_ref = refs[pos] if has_res else None
    pos += int(has_res)
    o_ref = refs[pos]
    wb_refs = refs[pos + 1:] if cast else [r[0] for r in w_refs]

    if cast:
        @pl.when(pl.program_id(1) == 0)
        def _():
            for prefs, pcs, tr, wb_ref in zip(w_refs, pieces, transposed, wb_refs):
                ax = 0 if tr else 1
                cut = (lambda v, c0, c1: v[c0:c1, :]) if tr else (lambda v, c0, c1: v[:, c0:c1])
                parts = [cut(r[...], c0, c1) if (c0, c1) != (0, bw) else r[...] for r, (bw, _, _, c0, c1) in zip(prefs, pcs)]
                fill = wb_ref.shape[ax] - sum(c1 - c0 for (_, _, _, c0, c1) in pcs)
                if fill:
                    shape = (fill, wb_ref.shape[1]) if tr else (wb_ref.shape[0], fill)
                    parts.append(jnp.zeros(shape, F32))
                tile = parts[0] if len(parts) == 1 else jnp.concatenate(parts, axis=ax)
                wb_ref[...] = tile.astype(BF16)

    accs = [(_dot_nt if tr else functools.partial(jnp.dot, preferred_element_type=F32))(x_refs[xi][...], wb[...])
            for xi, wb, tr in zip(x_of_w, wb_refs, transposed)]
    res = res_ref[...] if has_res else None
    o_ref[...] = epilogue(accs, res).astype(o_ref.dtype)


def _mm_row_tile(mp, xs, ws, tn, out_dtype, has_res, x_stationary):
    fixed = 0
    for wspec in ws:
        w, k = wspec[0], wspec[2]
        fixed += 2 * k * sum(p[0] for p in _w_pieces(wspec, tn)) * w.dtype.itemsize
        fixed += k * tn * 2 if w.dtype != BF16 else 0
    per_row = sum(x.shape[1] * x.dtype.itemsize * (1 if x_stationary else 2) for x in xs)
    per_row += 2 * tn * jnp.dtype(out_dtype).itemsize + (2 * tn * 4 if has_res else 0)
    per_row += len(ws) * tn * 4
    best = SAMPLE_PAD
    for tm in range(SAMPLE_PAD, MM_MAX_ROWS + 1, SAMPLE_PAD):
        if mp % tm == 0 and fixed + tm * per_row <= MM_VMEM_BUDGET:
            best = tm
    return best


def _matmul(xs, ws, x_of_w, epilogue, n_out, out_dtype, tn, name, res=None, x_stationary=False):
    mp = xs[0].shape[0]
    assert n_out % tn == 0
    cast = not x_stationary
    assert all((wspec[0].dtype == BF16) == x_stationary for wspec in ws)
    tm = _mm_row_tile(mp, xs, ws, tn, out_dtype, res is not None, x_stationary)
    if x_stationary:
        grid = (mp // tm, n_out // tn)
        ij = lambda a, b: (a, b)
    else:
        grid = (n_out // tn, mp // tm)
        ij = lambda a, b: (b, a)
    x_mode = dict(pipeline_mode=pl.Buffered(1)) if x_stationary else {}
    in_specs = [pl.BlockSpec((tm, x.shape[1]), lambda a, b: (ij(a, b)[0], 0), **x_mode) for x in xs]
    arrs = list(xs)
    scratch = []
    pieces = [_w_pieces(wspec, tn) for wspec in ws]
    transposed = [_w_transposed(wspec) for wspec in ws]
    for wspec, pcs, tr in zip(ws, pieces, transposed):
        w, layer, k, rb = wspec[:4]
        assert (len(pcs) == 1 and not tr) or cast
        for (bw, c0, step, _, _) in pcs:
            col = functools.partial(lambda a, b, c0, step: c0 + step * ij(a, b)[1], c0=c0, step=step)
            if tr:
                in_specs.append(pl.BlockSpec((None, bw, k), functools.partial(
                    lambda a, b, layer, rb, col: (layer, col(a, b), rb), layer=layer, rb=rb, col=col)))
            else:
                in_specs.append(pl.BlockSpec((None, k, bw), functools.partial(
                    lambda a, b, layer, rb, col: (layer, rb, col(a, b)), layer=layer, rb=rb, col=col)))
            arrs.append(w)
        if cast:
            scratch.append(pltpu.VMEM((tn, k) if tr else (k, tn), BF16))
    if res is not None:
        in_specs.append(pl.BlockSpec((tm, tn), lambda a, b: ij(a, b)))
        arrs.append(res)
    return pl.pallas_call(
        functools.partial(_mm_kernel, x_of_w=tuple(x_of_w), pieces=pieces, transposed=transposed, epilogue=epilogue,
                          has_res=res is not None, cast=cast),
        grid=grid,
        in_specs=in_specs,
        out_specs=pl.BlockSpec((tm, tn), lambda a, b: ij(a, b)),
        out_shape=jax.ShapeDtypeStruct((mp, n_out), out_dtype),
        scratch_shapes=scratch,
        compiler_params=_cparams(("arbitrary", "arbitrary")), name=name,
    )(*arrs)


def _col_tile(n, target):
    return _divisor_tile(n, target, 128)


def _dwconv_prompt_kernel(x_ref, w_ref, b_ref, o_ref, xs_ref, wb_ref, *, width, seq, pad, rows, act, batch):
    ct = x_ref.shape[1]

    @pl.when(pl.program_id(0) == batch)
    def _():
        o_ref[...] = jnp.zeros(o_ref.shape, F32)

    @pl.when(pl.program_id(0) < batch)
    def _():
        _dwconv_prompt_body(x_ref, w_ref, b_ref, o_ref, xs_ref, wb_ref, width=width, seq=seq, pad=pad, rows=rows,
                            act=act)


def _dwconv_prompt_body(x_ref, w_ref, b_ref, o_ref, xs_ref, wb_ref, *, width, seq, pad, rows, act):
    ct = x_ref.shape[1]
    xs_ref[0:pad, :] = jnp.zeros((pad, ct), F32)
    xs_ref[pad:pad + seq, :] = x_ref[...]
    bias = b_ref[...]
    for j in range(width):
        wb_ref[j] = jnp.broadcast_to(w_ref[pl.ds(j, 1), :], (8, ct))

    def body(i, carry):
        r0 = pl.multiple_of(i * rows, rows)
        acc = jnp.zeros((rows, ct), F32) + bias
        win = xs_ref[pl.ds(r0, rows + pad), :]
        off = pad - (width - 1)
        for r in range(8):
            taps = [j for j in range(width) if (off + j) % 8 == r]
            if not taps:
                continue
            wr = win if r == 0 else pltpu.roll(win, rows + pad - r, 0)
            for j in taps:
                base = off + j - r
                acc = acc + jnp.concatenate([wb_ref[j]] * (rows // 8), axis=0) * wr[base:base + rows]
        if act:
            acc = acc * _sigmoid(acc)
        o_ref[pl.ds(r0, rows), :] = acc
        return carry

    lax.fori_loop(0, seq // rows, body, 0)


def _dwconv_prompt(x, w, bias, col0, ncols, batch, seq, act, name):
    mp = x.shape[0]
    width = w.shape[0]
    ct = _col_tile(ncols, 512)
    assert col0 % ct == 0
    pad = -(-(width - 1) // 8) * 8
    rows = 32
    return pl.pallas_call(
        functools.partial(_dwconv_prompt_kernel, width=width, seq=seq, pad=pad, rows=rows, act=act, batch=batch),
        grid=(batch + 1, ncols // ct),
        in_specs=[pl.BlockSpec((seq, ct), lambda b, j: (jnp.minimum(b, batch - 1), col0 // ct + j)),
                  pl.BlockSpec((width, ct), lambda b, j: (0, j)),
                  pl.BlockSpec((1, ct), lambda b, j: (0, j))],
        out_specs=pl.BlockSpec((seq, ct), lambda b, j: (b, j)),
        out_shape=jax.ShapeDtypeStruct((mp, ncols), F32),
        scratch_shapes=[pltpu.VMEM((pad + seq, ct), F32), pltpu.VMEM((width, 8, ct), F32)],
        compiler_params=_cparams(("parallel", "parallel")), name=name,
    )(x, w, bias.reshape(1, ncols))


def _dwconv_sample_kernel(x_ref, h_ref, w_ref, b_ref, prev_ref, o_ref, *, width, nb, act):
    del prev_ref
    ct = x_ref.shape[1]
    acc = jnp.sum(h_ref[...] * w_ref[0:width - 1, :][None], axis=1)
    acc = acc + x_ref[0:nb, :] * w_ref[width - 1:width, :] + b_ref[...]
    if act:
        acc = acc * _sigmoid(acc)
    o_ref[0:nb, :] = acc
    o_ref[nb:, :] = jnp.zeros((SAMPLE_PAD - nb, ct), F32)


def _dwconv_sample(x, hist, w, bias, col0, ncols, prev, row0, act, name):
    nb = hist.shape[0]
    width = w.shape[0]
    ct = _col_tile(ncols, 512)
    rb = row0 // SAMPLE_PAD
    return pl.pallas_call(
        functools.partial(_dwconv_sample_kernel, width=width, nb=nb, act=act),
        grid=(ncols // ct,),
        in_specs=[pl.BlockSpec((SAMPLE_PAD, ct), lambda j: (rb, col0 // ct + j)),
                  pl.BlockSpec((nb, width - 1, ct), lambda j: (0, 0, j)),
                  pl.BlockSpec((width, ct), lambda j: (0, j)),
                  pl.BlockSpec((1, ct), lambda j: (0, j)),
                  pl.BlockSpec(memory_space=pl.ANY)],
        out_specs=pl.BlockSpec((SAMPLE_PAD, ct), lambda j: (rb, j)),
        out_shape=jax.ShapeDtypeStruct(prev.shape, F32),
        input_output_aliases={4: 0},
        compiler_params=_cparams(("parallel",)), name=name,
    )(x, hist, w, bias.reshape(1, ncols), prev)


def _pool_prompt_kernel(x_ref, w_ref, o_ref, *, nblk):
    x = x_ref[...].reshape(nblk, NSA_BLOCK, x_ref.shape[1])
    o_ref[...] = jnp.sum(x * w_ref[...][None], axis=1)


def _pool_prompt(proj, w_exp, batch, seq):
    nblk = seq // NSA_BLOCK
    return pl.pallas_call(
        functools.partial(_pool_prompt_kernel, nblk=nblk),
        grid=(batch,),
        in_specs=[pl.BlockSpec((seq, 2 * KVW), lambda b: (b, QW // (2 * KVW))),
                  pl.BlockSpec((NSA_BLOCK, 2 * KVW), lambda b: (0, 0))],
        out_specs=pl.BlockSpec((None, nblk, 2 * KVW), lambda b: (b, 0, 0)),
        out_shape=jax.ShapeDtypeStruct((batch, nblk, 2 * KVW), F32),
        compiler_params=_cparams(("parallel",)), name="nsa_pool_prompt",
    )(proj, w_exp)


def _nsa_prompt_kernel(sl_ref, q_ref, kc_ref, vc_ref, ks_ref, vs_ref, kw_ref, vw_ref, gt_ref, o_ref, *, tq, seq, batch):
    @pl.when(pl.program_id(0) == batch)
    def _():
        o_ref[...] = jnp.zeros(o_ref.shape, o_ref.dtype)

    @pl.when(pl.program_id(0) < batch)
    def _():
        _nsa_prompt_body(sl_ref, q_ref, kc_ref, vc_ref, ks_ref, vs_ref, kw_ref, vw_ref, gt_ref, o_ref, tq=tq, seq=seq)


def _nsa_prompt_body(sl_ref, q_ref, kc_ref, vc_ref, ks_ref, vs_ref, kw_ref, vw_ref, gt_ref, o_ref, *, tq, seq):
    g = pl.program_id(1)
    p0 = pl.program_id(2) * tq
    nblk = seq // NSA_BLOCK
    n_keep = min(NSA_TOPK, nblk)
    scale = HEAD_DIM ** -0.5
    rr = NSA_GROUP
    slopes = [sl_ref[g, r] for r in range(rr)]
    heads = [slice(r * tq, (r + 1) * tq) for r in range(rr)]
    tn_dims = (((0,), (0,)), ((), ()))

    q = q_ref[...]
    qs = jnp.concatenate([q[:, r * HEAD_DIM:(r + 1) * HEAD_DIM] for r in range(rr)], axis=0).astype(BF16)
    pos = p0 + _iota((tq, 1), 0)

    kc = kc_ref[...].astype(BF16)
    vc = vc_ref[...].astype(BF16)
    pos_l = p0 + _iota((1, tq), 1)
    blk = _iota((nblk, 1), 0)
    blk_end = blk * NSA_BLOCK + (NSA_BLOCK - 1)
    valid_c = blk_end <= pos_l
    dist_c = (pos_l - blk_end).astype(F32)
    st = _dot_nt(kc, qs) * scale
    probs = []
    for r in range(rr):
        sr = jnp.where(valid_c, st[:, heads[r]] - slopes[r] * dist_c, NEG)
        e = jnp.where(valid_c, jnp.exp(sr - jnp.max(sr, axis=0, keepdims=True)), 0.0)
        probs.append(e / jnp.maximum(jnp.sum(e, axis=0, keepdims=True), 1e-30))
    pt = jnp.concatenate(probs, axis=1)
    o_c = lax.dot_general(pt.astype(BF16), vc, tn_dims, preferred_element_type=F32)

    imp = probs[0]
    for r in range(1, rr):
        imp = imp + probs[r]
    cur = pos_l // NSA_BLOCK
    forced = (blk == 0) | (blk == cur) | (blk == cur - 1)
    imp = jnp.where(forced, NSA_FORCE, imp)
    imp = jnp.where(blk * NSA_BLOCK > pos_l, -1.0, imp)
    rank = jnp.zeros((nblk, tq), I32)
    for i in range(nblk):
        c = imp[i:i + 1, :]
        rank = rank + ((c > imp) | ((c == imp) & (blk > i))).astype(I32)
    sel_t = (rank < n_keep).astype(BF16)

    kch = min(SEL_KCHUNK, seq)
    n_chunks = (p0 + tq + kch - 1) // kch

    def sel_body(c, carry):
        m, l, acc = carry
        k0 = pl.multiple_of(c * kch, kch)
        kk = ks_ref[pl.ds(k0, kch), :].astype(BF16)
        vv = vs_ref[pl.ds(k0, kch), :].astype(BF16)
        kpos = k0 + _iota((1, kch), 1)
        expand = (_iota((nblk, kch), 0) == ((k0 + _iota((nblk, kch), 1)) // NSA_BLOCK)).astype(BF16)
        selk = lax.dot_general(sel_t, expand, tn_dims, preferred_element_type=F32) > 0.5
        mask = selk & (kpos <= pos)
        kb = (kpos - p0).astype(F32)
        qk = _dot_nt(qs, kk) * scale
        sc = jnp.concatenate([jnp.where(mask, qk[heads[r]] + slopes[r] * kb, NEG) for r in range(rr)], axis=0)
        m_new = jnp.maximum(m, jnp.max(sc, axis=-1, keepdims=True))
        alpha = jnp.exp(m - m_new)
        e = jnp.exp(sc - m_new)
        l = alpha * l + jnp.sum(e, axis=-1, keepdims=True)
        acc = alpha * acc + jnp.dot(e.astype(BF16), vv, preferred_element_type=F32)
        return m_new, l, acc

    init = (jnp.full((rr * tq, 1), NEG, F32), jnp.zeros((rr * tq, 1), F32), jnp.zeros((rr * tq, HEAD_DIM), F32))
    _, l, acc = lax.fori_loop(0, n_chunks, sel_body, init)
    o_s = acc / jnp.maximum(l, 1e-30)

    span = min(NSA_WINDOW + tq, seq)
    w0 = pl.multiple_of(jnp.clip(p0 + tq - span, 0, seq - span), 128)
    kk = kw_ref[pl.ds(w0, span), :].astype(BF16)
    vv = vw_ref[pl.ds(w0, span), :].astype(BF16)
    kpos = w0 + _iota((1, span), 1)
    dist = pos - kpos
    valid_w = (dist >= 0) & (dist < NSA_WINDOW)
    kb = (kpos - p0).astype(F32)
    qk = _dot_nt(qs, kk) * scale
    ews, lws = [], []
    for r in range(rr):
        sr = jnp.where(valid_w, qk[heads[r]] + slopes[r] * kb, NEG)
        e = jnp.exp(sr - jnp.max(sr, axis=-1, keepdims=True))
        ews.append(e)
        lws.append(jnp.sum(e, axis=-1, keepdims=True))
    o_w = jnp.dot(jnp.concatenate(ews, axis=0).astype(BF16), vv, preferred_element_type=F32)

    gt = _sigmoid(gt_ref[...])
    for r in range(rr):
        o_ref[:, r * HEAD_DIM:(r + 1) * HEAD_DIM] = (gt[:, 3 * r:3 * r + 1] * o_c[heads[r]]
                                                    + gt[:, 3 * r + 1:3 * r + 2] * o_s[heads[r]]
                                                    + gt[:, 3 * r + 2:3 * r + 3] * (o_w[heads[r]] / lws[r])
                                                    ).astype(o_ref.dtype)


def _nsa_prompt(proj, kcvc, gates_t, slopes, batch, seq):
    mp = proj.shape[0]
    tq = min(NSA_QTILE, seq)
    nq = seq // tq
    nblk = seq // NSA_BLOCK
    hb = HEAD_DIM
    c_sel = (QW + 2 * KVW) // hb
    c_win = (QW + 4 * KVW) // hb
    g4 = NSA_KV_HEADS
    bb = lambda b: jnp.minimum(b, batch - 1)
    row = lambda b, i: jnp.where(b < batch, b * nq + i, batch * nq)
    return pl.pallas_call(
        functools.partial(_nsa_prompt_kernel, tq=tq, seq=seq, batch=batch),
        grid=(batch + 1, NSA_KV_HEADS, nq),
        in_specs=[pl.BlockSpec(memory_space=pltpu.SMEM),
                  pl.BlockSpec((tq, GW), lambda b, g, i: (bb(b) * nq + i, g)),
                  pl.BlockSpec((None, nblk, hb), lambda b, g, i: (bb(b), 0, g)),
                  pl.BlockSpec((None, nblk, hb), lambda b, g, i: (bb(b), 0, g4 + g)),
                  pl.BlockSpec((seq, hb), lambda b, g, i: (bb(b), c_sel + g)),
                  pl.BlockSpec((seq, hb), lambda b, g, i: (bb(b), c_sel + g4 + g)),
                  pl.BlockSpec((seq, hb), lambda b, g, i: (bb(b), c_win + g)),
                  pl.BlockSpec((seq, hb), lambda b, g, i: (bb(b), c_win + g4 + g)),
                  pl.BlockSpec((None, tq, 3 * NSA_GROUP), lambda b, g, i: (g, bb(b) * nq + i, 0))],
        out_specs=pl.BlockSpec((tq, GW), lambda b, g, i: (row(b, i), g)),
        out_shape=jax.ShapeDtypeStruct((mp, QW), BF16),
        compiler_params=_cparams(("parallel", "parallel", "arbitrary")), name="nsa_prompt",
    )(slopes, proj, kcvc, kcvc, proj, proj, proj, proj, gates_t)


def _kv_state_kernel(*refs):
    x_refs, o_ref = refs[:-1], refs[-1]
    for layer, x_ref in enumerate(x_refs):
        @pl.when(pl.program_id(0) == layer)
        def _(x_ref=x_ref):
            x = x_ref[...]
            for i in range(o_ref.shape[1]):
                o_ref[:, i, :] = x[:, i * HEAD_DIM:(i + 1) * HEAD_DIM]


def _kv_state(projs, col0, ncols, n_rows):
    tr = _divisor_tile(n_rows, 512, 8)
    assert col0 % ncols == 0
    n = ncols // HEAD_DIM
    spec = lambda layer: pl.BlockSpec((tr, ncols), lambda e, i: (jnp.where(e == layer, i, 0), col0 // ncols))
    return pl.pallas_call(
        _kv_state_kernel, grid=(len(projs), n_rows // tr),
        in_specs=[spec(layer) for layer in range(len(projs))],
        out_specs=pl.BlockSpec((None, tr, n, HEAD_DIM), lambda e, i: (e, i, 0, 0)),
        out_shape=jax.ShapeDtypeStruct((len(projs), n_rows, n, HEAD_DIM), F32),
        compiler_params=_cparams(("parallel", "parallel")), name="nsa_kv_state",
    )(*projs)


def _pool_sample_kernel(pt_ref, *refs, n_page):
    del pt_ref
    c_refs = refs[:n_page]
    wk_ref, wv_ref, ko_ref, vo_ref = refs[n_page:]
    per = c_refs[0].shape[0] // NSA_BLOCK
    for t, c_ref in enumerate(c_refs):
        for half in range(per):
            rows = pl.ds(half * NSA_BLOCK, NSA_BLOCK)
            ko_ref[t * per + half] = jnp.sum(c_ref[rows, 0] * wk_ref[...], axis=0)
            vo_ref[t * per + half] = jnp.sum(c_ref[rows, 1] * wv_ref[...], axis=0)


def _pool_sample(cache, e, page_table, wk, wv):
    nd, n_pages = page_table.shape
    page_rows = cache.shape[2]
    per = page_rows // NSA_BLOCK
    kvh, hd = cache.shape[4], cache.shape[5]
    n_page = math.gcd(n_pages, POOL_PAGES_PER_STEP)
    out = jax.ShapeDtypeStruct((nd, n_pages * per, kvh, hd), F32)
    page_spec = lambda t: pl.BlockSpec((None, None, page_rows, 2, kvh, hd),
                                       lambda b, p, pt: (e, pt[b, p * n_page + t], 0, 0, 0, 0))
    grid_spec = pltpu.PrefetchScalarGridSpec(
        num_scalar_prefetch=1, grid=(nd, n_pages // n_page),
        in_specs=[page_spec(t) for t in range(n_page)]
                 + [pl.BlockSpec((NSA_BLOCK, kvh, hd), lambda b, p, pt: (0, 0, 0))] * 2,
        out_specs=[pl.BlockSpec((None, n_page * per, kvh, hd), lambda b, p, pt: (b, p, 0, 0))] * 2)
    return pl.pallas_call(functools.partial(_pool_sample_kernel, n_page=n_page), grid_spec=grid_spec,
                          out_shape=[out, out],
                          compiler_params=_cparams(("parallel", "parallel")), name="nsa_pool_sample",
                          )(page_table, *([cache] * n_page), wk, wv)


def _head_rows(row, n):
    return jnp.concatenate([row[:, h * HEAD_DIM:(h + 1) * HEAD_DIM] for h in range(n)], axis=0)


def _nsa_cmp_sample_kernel(sl_ref, q_ref, kc_ref, vc_ref, oc_ref, oh_ref, *, past, n_sel):
    b = pl.program_id(0)
    nblk = kc_ref.shape[0]
    scale = HEAD_DIM ** -0.5
    rr = NSA_GROUP
    qh = _head_rows(q_ref[pl.ds(b, 1), :], NSA_HEADS).astype(BF16)
    blk = _iota((1, nblk), 1)
    dist = (past - (blk * NSA_BLOCK + (NSA_BLOCK - 1))).astype(F32)
    last = (past // NSA_BLOCK) - 1
    for g in range(NSA_KV_HEADS):
        qg = jnp.concatenate([qh[g * rr:(g + 1) * rr], jnp.zeros((8 - rr, HEAD_DIM), BF16)], axis=0)
        slope = jnp.concatenate([jnp.full((1, 1), sl_ref[g, r], F32) for r in range(rr)]
                                + [jnp.zeros((8 - rr, 1), F32)], axis=0)
        kc = kc_ref[:, g, :].astype(BF16)
        vc = vc_ref[:, g, :].astype(BF16)
        s = _dot_nt(qg, kc) * scale - slope * dist
        p = _msoftmax(s, blk >= 0)
        oc_ref[g * 8:(g + 1) * 8, :] = jnp.dot(p.astype(BF16), vc, preferred_element_type=F32)
        imp = jnp.sum(p[0:rr], axis=0, keepdims=True)
        imp = jnp.where((blk == 0) | (blk == last), NSA_FORCE, imp)
        impb = jnp.broadcast_to(imp, (nblk, nblk))
        eye = _iota((nblk, nblk), 0) == _iota((nblk, nblk), 1)
        impc = jnp.sum(jnp.where(eye, impb, 0.0), axis=1, keepdims=True)
        before = (impc > impb) | ((impc == impb) & (_iota((nblk, nblk), 0) < _iota((nblk, nblk), 1)))
        rank = jnp.sum(before.astype(F32), axis=0, keepdims=True)
        sel = rank < n_sel
        lower = (_iota((nblk, nblk), 0) < _iota((nblk, nblk), 1)).astype(BF16)
        slot = jnp.dot(jnp.broadcast_to(sel.astype(BF16), (8, nblk)), lower, preferred_element_type=F32)[0:1]
        onehot = sel & (jnp.abs(slot - _iota((NSA_TOPK, nblk), 0).astype(F32)) < 0.5)
        idx = jnp.sum(jnp.where(onehot, _iota((NSA_TOPK, nblk), 1), 0), axis=1, keepdims=True)
        oh_ref[g] = jnp.broadcast_to(idx, (NSA_TOPK, HEAD_DIM))


def _nsa_cmp_sample(proj, kcs, vcs, slopes, row0, past):
    nd, nblk = kcs.shape[0], kcs.shape[1]
    n_sel = min(NSA_TOPK, nblk + 1) - 1
    assert past % NSA_BLOCK == 0 and nblk >= NSA_TOPK
    rb = row0 // SAMPLE_PAD
    return pl.pallas_call(
        functools.partial(_nsa_cmp_sample_kernel, past=past, n_sel=n_sel),
        grid=(nd,),
        in_specs=[pl.BlockSpec(memory_space=pltpu.SMEM),
                  pl.BlockSpec((SAMPLE_PAD, QW), lambda b: (rb, 0)),
                  pl.BlockSpec((None, nblk, NSA_KV_HEADS, HEAD_DIM), lambda b: (b, 0, 0, 0)),
                  pl.BlockSpec((None, nblk, NSA_KV_HEADS, HEAD_DIM), lambda b: (b, 0, 0, 0))],
        out_specs=[pl.BlockSpec((None, 8 * NSA_KV_HEADS, HEAD_DIM), lambda b: (b, 0, 0)),
                   pl.BlockSpec((None, NSA_KV_HEADS, NSA_TOPK, HEAD_DIM), lambda b: (b, 0, 0, 0))],
        out_shape=[jax.ShapeDtypeStruct((nd, 8 * NSA_KV_HEADS, HEAD_DIM), F32),
                   jax.ShapeDtypeStruct((nd, NSA_KV_HEADS, NSA_TOPK, HEAD_DIM), I32)],
        compiler_params=_cparams(("parallel",)), name="nsa_cmp_sample",
    )(slopes, proj, kcs, vcs)


def _nsa_sel_sample_kernel(idx_ref, pt_ref, sl_ref, proj_ref, c0_ref, c1_ref, c2_ref, c3_ref, win_ref, oc_ref,
                           gt_ref, prev_ref, o_ref, m_ref, l_ref, acc_ref, rows_ref, *, past, n_sel, nd):
    del pt_ref, prev_ref
    b = pl.program_id(0)
    j = pl.program_id(1)
    scale = HEAD_DIM ** -0.5
    rr = NSA_GROUP
    c_refs = (c0_ref, c1_ref, c2_ref, c3_ref)

    @pl.when((b == 0) & (j == 0))
    def _():
        rows_ref[...] = jnp.zeros(rows_ref.shape, F32)

    @pl.when(j == 0)
    def _():
        m_ref[...] = jnp.full(m_ref.shape, NEG, F32)
        l_ref[...] = jnp.zeros(l_ref.shape, F32)
        acc_ref[...] = jnp.zeros(acc_ref.shape, F32)

    row = proj_ref[pl.ds(b, 1), :]
    qh = _head_rows(row[:, 0:QW], NSA_HEADS).astype(BF16)

    def group_q(g):
        qg = jnp.concatenate([qh[g * rr:(g + 1) * rr], jnp.zeros((8 - rr, HEAD_DIM), BF16)], axis=0)
        slope = jnp.concatenate([jnp.full((1, 1), sl_ref[g, r], F32) for r in range(rr)]
                                + [jnp.zeros((8 - rr, 1), F32)], axis=0)
        return qg, slope

    groups = range(NSA_KV_HEADS)
    q4 = jnp.stack([group_q(g)[0] for g in groups])
    slope4 = jnp.stack([group_q(g)[1] for g in groups])
    kk = jnp.stack([c_refs[g][:, 0, g, :] for g in groups]).astype(BF16)
    vv = jnp.stack([c_refs[g][:, 1, g, :] for g in groups]).astype(BF16)
    blk4 = jnp.stack([jnp.full((1, 1), idx_ref[(b * NSA_KV_HEADS + g) * NSA_TOPK + j], I32) for g in groups])
    dist = (past - (blk4 * NSA_BLOCK + _iota((1, 1, NSA_BLOCK), 2))).astype(F32)
    sc = _bmm_nt(q4, kk) * scale - slope4 * dist
    m_old = m_ref[:, :, 0:1]
    m_new = jnp.maximum(m_old, jnp.max(sc, axis=-1, keepdims=True))
    alpha = jnp.exp(m_old - m_new)
    e = jnp.exp(sc - m_new)
    l_ref[...] = alpha * l_ref[...] + jnp.sum(e, axis=-1, keepdims=True)
    acc_ref[...] = alpha * acc_ref[...] + _bmm(e.astype(BF16), vv)
    m_ref[...] = jnp.broadcast_to(m_new, m_ref.shape)

    @pl.when(j == n_sel - 1)
    def _():
        gt = _sigmoid(gt_ref[pl.ds(b, 1), :])
        pieces = []
        for g in range(NSA_KV_HEADS):
            qg, slope = group_q(g)
            qf = qg.astype(F32)
            base = QW + g * HEAD_DIM
            k_sel = row[:, base + 2 * KVW: base + 2 * KVW + HEAD_DIM].astype(BF16).astype(F32)
            v_sel = row[:, base + 3 * KVW: base + 3 * KVW + HEAD_DIM].astype(BF16).astype(F32)
            k_win = row[:, base + 4 * KVW: base + 4 * KVW + HEAD_DIM].astype(BF16).astype(F32)
            v_win = row[:, base + 5 * KVW: base + 5 * KVW + HEAD_DIM].astype(BF16).astype(F32)
            rows = slice(g * 8, (g + 1) * 8)
            s_new = jnp.sum(qf * k_sel, axis=-1, keepdims=True) * scale
            m_old = m_ref[g, :, 0:1]
            m_new = jnp.maximum(m_old, s_new)
            alpha = jnp.exp(m_old - m_new)
            e_new = jnp.exp(s_new - m_new)
            l = alpha * l_ref[g, :, 0:1] + e_new
            o_s = (alpha * acc_ref[g] + e_new.astype(BF16).astype(F32) * v_sel) / jnp.maximum(l, 1e-30)
            nwin = win_ref.shape[0]
            kw = win_ref[:, 0, g, :].astype(BF16)
            vw = win_ref[:, 1, g, :].astype(BF16)
            dist = nwin - _iota((1, nwin), 1)
            sw = _dot_nt(qg, kw) * scale - slope * dist.astype(F32)
            valid = dist < NSA_WINDOW
            sw = jnp.where(valid, sw, NEG)
            sw_new = jnp.sum(qf * k_win, axis=-1, keepdims=True) * scale
            mw = jnp.maximum(jnp.max(sw, axis=-1, keepdims=True), sw_new)
            ew = jnp.where(valid, jnp.exp(sw - mw), 0.0)
            ew_new = jnp.exp(sw_new - mw)
            den = jnp.maximum(jnp.sum(ew, axis=-1, keepdims=True) + ew_new, 1e-30)
            o_w = (jnp.dot((ew / den).astype(BF16), vw, preferred_element_type=F32)
                   + (ew_new / den).astype(BF16).astype(F32) * v_win)
            o_c = oc_ref[rows, :]
            for r in range(rr):
                c = g * 3 * rr + 3 * r
                pieces.append(gt[:, c:c + 1] * o_c[r:r + 1] + gt[:, c + 1:c + 2] * o_s[r:r + 1]
                              + gt[:, c + 2:c + 3] * o_w[r:r + 1])
        rows_ref[pl.ds(b, 1), :] = jnp.concatenate(pieces, axis=1)

    @pl.when((b == nd - 1) & (j == n_sel - 1))
    def _():
        o_ref[...] = rows_ref[...].astype(o_ref.dtype)


def _nsa_sel_sample(proj, small, cache, win_cache, e, page_table, idx, o_cmp, slopes, prev, row0, past):
    nd = page_table.shape[0]
    page_rows = cache.shape[2]
    per = page_rows // NSA_BLOCK
    kvh, hd = cache.shape[4], cache.shape[5]
    nwin = win_cache.shape[2]
    n_sel = NSA_TOPK - 1
    rb = row0 // SAMPLE_PAD

    def cache_spec(g):
        def imap(b, j, idx_r, pt_r):
            blk = idx_r[(b * NSA_KV_HEADS + g) * NSA_TOPK + j]
            return (e, pt_r[b, blk // per], blk % per, 1, 0, 0)
        return pl.BlockSpec((None, None, NSA_BLOCK, 2, kvh, hd), imap)

    grid_spec = pltpu.PrefetchScalarGridSpec(
        num_scalar_prefetch=2, grid=(nd, n_sel),
        in_specs=[pl.BlockSpec(memory_space=pltpu.SMEM),
                  pl.BlockSpec((SAMPLE_PAD, proj.shape[1]), lambda b, j, i_r, p_r: (rb, 0)),
                  cache_spec(0), cache_spec(1), cache_spec(2), cache_spec(3),
                  pl.BlockSpec((None, None, nwin, 2, kvh, hd), lambda b, j, i_r, p_r: (e, b, 0, 0, 0, 0)),
                  pl.BlockSpec((None, 8 * NSA_KV_HEADS, HEAD_DIM), lambda b, j, i_r, p_r: (b, 0, 0)),
                  pl.BlockSpec((SAMPLE_PAD, small.shape[1]), lambda b, j, i_r, p_r: (rb, 0)),
                  pl.BlockSpec(memory_space=pl.ANY)],
        out_specs=pl.BlockSpec((SAMPLE_PAD, QW), lambda b, j, i_r, p_r: (rb, 0)),
        scratch_shapes=[pltpu.VMEM((NSA_KV_HEADS, 8, HEAD_DIM), F32)] * 3 + [pltpu.VMEM((SAMPLE_PAD, QW), F32)])
    return pl.pallas_call(
        functools.partial(_nsa_sel_sample_kernel, past=past, n_sel=n_sel, nd=nd),
        grid_spec=grid_spec,
        out_shape=jax.ShapeDtypeStruct(prev.shape, prev.dtype),
        input_output_aliases={11: 0},
        compiler_params=_cparams(("arbitrary", "arbitrary")), name="nsa_sel_sample",
    )(idx, page_table, slopes, proj, cache, cache, cache, cache, win_cache, o_cmp, small, prev)


def _stack_heads(x, n):
    return jnp.stack([x[:, h * HEAD_DIM:(h + 1) * HEAD_DIM] for h in range(n)], axis=0)


def _l2n(x):
    return x * lax.rsqrt(jnp.sum(x * x, axis=-1, keepdims=True) + EPS)


def _softplus(x):
    return jnp.maximum(x, 0.0) + jnp.log(1.0 + jnp.exp(-jnp.abs(x)))


def _bmm(a, b, **kw):
    return jnp.einsum('hcs,hsd->hcd', a, b, preferred_element_type=F32, **kw)


def _bmm_nt(a, b, **kw):
    return jnp.einsum('hcd,hsd->hcs', a, b, preferred_element_type=F32, **kw)


def _bmm_tn(a, b, **kw):
    return jnp.einsum('hcd,hce->hde', a, b, preferred_element_type=F32, **kw)


def _split_bf16(x):
    hi = x.astype(BF16)
    return hi, (x - hi.astype(F32)).astype(BF16)


def _bmm_split(ah, al, bh, bl):
    return _bmm(ah, bh) + (_bmm(ah, bl) + _bmm(al, bh))


def _gdn_prompt_kernel(x_ref, sm_ref, al_ref, dt_ref, z_ref, g_ref, o_ref, s_ref, *, a_col, b_col, batch):
    @pl.when(pl.program_id(0) == batch)
    def _():
        o_ref[...] = jnp.zeros(o_ref.shape, o_ref.dtype)

    @pl.when(pl.program_id(0) < batch)
    def _():
        _gdn_chunk(x_ref, sm_ref, al_ref, dt_ref, z_ref, g_ref, o_ref, s_ref, a_col=a_col, b_col=b_col)


def _gdn_chunk(x_ref, sm_ref, al_ref, dt_ref, z_ref, g_ref, o_ref, s_ref, *, a_col, b_col):
    nh, cc = GDN_HEADS, GDN_CHUNK
    nc = x_ref.shape[0] // cc
    nb = nc * nh
    rows = [slice(c * cc, (c + 1) * cc) for c in range(nc)]
    x = x_ref[...]
    stack = lambda c0: jnp.concatenate([_stack_heads(x[r, c0:c0 + GH], nh) for r in rows], axis=0)
    q = _l2n(stack(0)) * (HEAD_DIM ** -0.5)
    k = _l2n(stack(GH))
    v = stack(2 * GH)
    sm = sm_ref[...]
    gdec = -jnp.exp(al_ref[...]) * _softplus(sm + dt_ref[...])
    ri, ci = _iota((cc, cc), 0), _iota((cc, cc), 1)
    tri = ri >= ci
    strict = ri > ci
    gcum = [jnp.dot(tri.astype(F32), gdec[r], preferred_element_type=F32, precision=HIGHEST) for r in rows]
    beta = _sigmoid(sm)
    gc = jnp.stack([gcum[c][:, a_col + h:a_col + h + 1] for c in range(nc) for h in range(nh)], axis=0)
    bc = jnp.stack([beta[r][:, b_col + h:b_col + h + 1] for r in rows for h in range(nh)], axis=0)
    gr = jnp.sum(jnp.where((ri == ci)[None], jnp.broadcast_to(gc, (nb, cc, cc)), 0.0), axis=1, keepdims=True)
    decay = jnp.where(tri[None], jnp.exp(jnp.where(tri[None], gc - gr, 0.0)), 0.0)
    kb = k * bc
    vb = v * bc
    a = jnp.where(strict[None], _bmm_nt(kb, k) * decay, 0.0)
    neg = -a
    tm = jnp.where((ri == ci)[None], 1.0, 0.0) + neg
    pw = neg
    span = 2
    while span < cc:
        ph, plo = _split_bf16(pw)
        pw = _bmm_split(ph, plo, ph, plo)
        ph, plo = _split_bf16(pw)
        th, tlo = _split_bf16(tm)
        tm = tm + _bmm_split(th, tlo, ph, plo)
        span *= 2
    eg = jnp.exp(gc)
    gl = gc[:, cc - 1:cc, :]
    u = _bmm(tm, vb)
    w = _bmm(tm, kb * eg)
    qk = jnp.where(tri[None], _bmm_nt(q, k) * decay, 0.0)
    qe = q * eg
    ke = k * jnp.exp(gl - gc)
    egl = jnp.exp(gl)

    @pl.when(pl.program_id(1) == 0)
    def _():
        s_ref[...] = jnp.zeros(s_ref.shape, F32)

    s = s_ref[...]
    for c in range(nc):
        hs = slice(c * nh, (c + 1) * nh)
        v_new = u[hs] - _bmm(w[hs], s)
        o = _bmm(qe[hs], s) + _bmm(qk[hs], v_new)
        s = s * egl[hs] + _bmm_tn(ke[hs], v_new)
        z = _stack_heads(z_ref[rows[c], :], nh)
        o = _rms(o) * g_ref[...][None] * (z * _sigmoid(z))
        for h in range(nh):
            o_ref[rows[c], h * HEAD_DIM:(h + 1) * HEAD_DIM] = o[h].astype(o_ref.dtype)
    s_ref[...] = s


def _gdn_prompt(qkv_c, small, a_log_row, dt_row, gz, gdn_g, mp, batch, seq, a_col, b_col):
    assert GDN_CHUNK <= HEAD_DIM
    cc = GDN_CHUNK * math.gcd(seq // GDN_CHUNK, GDN_CHUNKS_PER_STEP)
    nch = seq // cc
    bb = lambda b: jnp.minimum(b, batch - 1)
    row = lambda b, n: jnp.where(b < batch, b * nch + n, batch * nch)
    vec = lambda w: pl.BlockSpec((1, w), lambda b, n: (0, 0))
    return pl.pallas_call(
        functools.partial(_gdn_prompt_kernel, a_col=a_col, b_col=b_col, batch=batch),
        grid=(batch + 1, nch),
        in_specs=[pl.BlockSpec((cc, 3 * GH), lambda b, n: (bb(b) * nch + n, 0)),
                  pl.BlockSpec((cc, small.shape[1]), lambda b, n: (bb(b) * nch + n, 0)),
                  vec(small.shape[1]), vec(small.shape[1]),
                  pl.BlockSpec((cc, GH), lambda b, n: (bb(b) * nch + n, 3)),
                  vec(HEAD_DIM)],
        out_specs=[pl.BlockSpec((cc, GH), lambda b, n: (row(b, n), 0)),
                   pl.BlockSpec((None, GDN_HEADS, HEAD_DIM, HEAD_DIM), lambda b, n: (bb(b), 0, 0, 0))],
        out_shape=[jax.ShapeDtypeStruct((mp, GH), BF16),
                   jax.ShapeDtypeStruct((batch, GDN_HEADS, HEAD_DIM, HEAD_DIM), F32)],
        compiler_params=_cparams(("parallel", "arbitrary")), name="gdn_prompt",
    )(qkv_c, small, a_log_row, dt_row, gz, gdn_g.reshape(1, HEAD_DIM))


def _col_of_row(row, n):
    eye = _iota((n, n), 0) == _iota((n, n), 1)
    return jnp.sum(jnp.where(eye, jnp.broadcast_to(row, (n, n)), 0.0), axis=1, keepdims=True)


def _gdn_sample_kernel(x_ref, h_ref, cw_ref, gz_ref, sm_ref, al_ref, dt_ref, g_ref, s0_ref, prev_ref,
                       o_ref, s_ref, rows_ref, *, a_col, b_col, nd):
    del prev_ref
    b = pl.program_id(0)
    nh = GDN_HEADS

    @pl.when(b == 0)
    def _():
        rows_ref[...] = jnp.zeros(rows_ref.shape, F32)

    nk = cw_ref.shape[0]
    conv = jnp.sum(h_ref[...] * cw_ref[0:nk - 1, :], axis=0, keepdims=True) + x_ref[pl.ds(b, 1), :] * cw_ref[nk - 1:nk, :]
    conv = conv * _sigmoid(conv)
    q = _l2n(_head_rows(conv[:, 0:GH], nh)) * (HEAD_DIM ** -0.5)
    k = _l2n(_head_rows(conv[:, GH:2 * GH], nh))
    v = _head_rows(conv[:, 2 * GH:3 * GH], nh)
    sm = sm_ref[pl.ds(b, 1), :]
    gdec = -jnp.exp(al_ref[...]) * _softplus(sm + dt_ref[...])
    eg = jnp.exp(_col_of_row(gdec[:, a_col:a_col + nh], nh))
    beta = _col_of_row(_sigmoid(sm)[:, b_col:b_col + nh], nh)
    s = s0_ref[...]
    pad = lambda t: jnp.concatenate([t[:, None, :], jnp.zeros((nh, 7, HEAD_DIM), F32)], axis=1)
    ks = _bmm(pad(k), s, precision=HIGHEST)[:, 0, :]
    v_new = beta * (v - eg * ks)
    o = eg * _bmm(pad(q), s, precision=HIGHEST)[:, 0, :] + jnp.sum(q * k, axis=-1, keepdims=True) * v_new
    s_ref[...] = s * eg[:, :, None] + _bmm_tn(pad(k), pad(v_new), precision=HIGHEST)
    z = _head_rows(gz_ref[pl.ds(b, 1), :], nh)
    o = _rms(o) * g_ref[...] * (z * _sigmoid(z))
    rows_ref[pl.ds(b, 1), :] = jnp.concatenate([o[h:h + 1] for h in range(nh)], axis=1)

    @pl.when(b == nd - 1)
    def _():
        o_ref[...] = rows_ref[...].astype(o_ref.dtype)


def _gdn_sample(gz, hist, conv_w, small, a_log_row, dt_row, gdn_g, s0, prev, row0, a_col, b_col):
    nd = hist.shape[0]
    rb = row0 // SAMPLE_PAD
    full = lambda shp: pl.BlockSpec(shp, lambda b: (0,) * len(shp))
    return pl.pallas_call(
        functools.partial(_gdn_sample_kernel, a_col=a_col, b_col=b_col, nd=nd),
        grid=(nd,),
        in_specs=[pl.BlockSpec((SAMPLE_PAD, 3 * GH), lambda b: (rb, 0)),
                  pl.BlockSpec((None, GDN_CONV - 1, 3 * GH), lambda b: (b, 0, 0)),
                  full(conv_w.shape),
                  pl.BlockSpec((SAMPLE_PAD, GH), lambda b: (rb, 3)),
                  pl.BlockSpec((SAMPLE_PAD, small.shape[1]), lambda b: (rb, 0)),
                  full(a_log_row.shape), full(dt_row.shape), full((1, HEAD_DIM)),
                  pl.BlockSpec((None, GDN_HEADS, HEAD_DIM, HEAD_DIM), lambda b: (b, 0, 0, 0)),
                  pl.BlockSpec(memory_space=pl.ANY)],
        out_specs=[pl.BlockSpec((SAMPLE_PAD, GH), lambda b: (rb, 0)),
                   pl.BlockSpec((None, GDN_HEADS, HEAD_DIM, HEAD_DIM), lambda b: (b, 0, 0, 0))],
        out_shape=[jax.ShapeDtypeStruct(prev.shape, prev.dtype), jax.ShapeDtypeStruct(s0.shape, F32)],
        scratch_shapes=[pltpu.VMEM((SAMPLE_PAD, GH), F32)],
        input_output_aliases={9: 0},
        compiler_params=_cparams(("arbitrary",)), name="gdn_sample",
    )(gz, hist, conv_w, gz, small, a_log_row, dt_row, gdn_g.reshape(1, HEAD_DIM), s0, prev)


def _even_layer(hn, e, dims, slopes, cache_nsa_kv, cache_nsa_win, state_gdn, state_gdn_conv, page_table,
                w_in, cmp_w, conv_w, a_log, dt_bias, gdn_g, w_out):
    batch, seq, nd, d = dims
    rows_p = batch * seq
    mp = hn.shape[0]
    past = page_table.shape[1] * cache_nsa_kv.shape[2]
    c_nsa = QW + 6 * KVW
    c_gate = c_nsa
    c_gdn = c_gate + 3 * NSA_HEADS
    c_a = c_gdn + 4 * GH
    a_col, b_col = 3 * NSA_HEADS, 3 * NSA_HEADS + GDN_HEADS
    hd = HEAD_DIM

    proj = _matmul([hn], [(w_in, e, d, 0, 0, None, True)], (0,), _ep_plain, c_nsa, F32, _col_tile(c_nsa, 512), "proj_nsa")
    tn = _col_tile(4 * GH, 512)
    off = c_gdn - c_nsa
    assert c_nsa % tn == 0 and 0 < off < hd and off % 8 == 0 and c_a % hd == off
    gdn_pieces = [(tn, c_nsa // tn, 1, off, tn), (hd, (c_nsa + tn) // hd, tn // hd, 0, off)]
    gz = _matmul([hn], [(w_in, e, d, 0, None, gdn_pieces, True)], (0,), _ep_plain, 4 * GH, F32, tn, "proj_gdn")
    small_pieces = [(hd, c_nsa // hd, 0, 0, off), (hd, c_a // hd, 0, off, off + 2 * GDN_HEADS)]
    small = _matmul([hn], [(w_in, e, d, 0, None, small_pieces, True)], (0,), _ep_plain, hd, F32, hd, "proj_small")

    w_exp = jnp.concatenate([jnp.repeat(cmp_w[:, :, 0], hd, axis=1), jnp.repeat(cmp_w[:, :, 1], hd, axis=1)], axis=1)
    kcvc = _pool_prompt(proj, w_exp, batch, seq)
    gates_t = small[:, 0:3 * NSA_HEADS].reshape(mp, NSA_KV_HEADS, 3 * NSA_GROUP).transpose(1, 0, 2)
    o_nsa = _nsa_prompt(proj, kcvc, gates_t, slopes, batch, seq)
    wk = jnp.broadcast_to(cmp_w[:, :, 0][:, :, None], (NSA_BLOCK, NSA_KV_HEADS, hd))
    wv = jnp.broadcast_to(cmp_w[:, :, 1][:, :, None], (NSA_BLOCK, NSA_KV_HEADS, hd))
    kcs, vcs = _pool_sample(cache_nsa_kv, e, page_table, wk, wv)
    o_cmp, sel_rows = _nsa_cmp_sample(proj, kcs, vcs, slopes, rows_p, past)
    idx = sel_rows[:, :, :, 0].reshape(-1)
    o_nsa = _nsa_sel_sample(proj, small, cache_nsa_kv, cache_nsa_win, e, page_table, idx, o_cmp, slopes, o_nsa,
                            rows_p, past)

    pad_row = lambda vec, col: jnp.zeros((1, hd), F32).at[0, col:col + GDN_HEADS].set(vec.astype(F32))
    a_log_row, dt_row = pad_row(a_log, a_col), pad_row(dt_bias, a_col)
    qkv_c = _dwconv_prompt(gz, conv_w, jnp.zeros((3 * GH,), F32), 0, 3 * GH, batch, seq, True, "gdn_conv_prompt")
    o_gdn, s_prompt = _gdn_prompt(qkv_c, small, a_log_row, dt_row, gz, gdn_g, mp, batch, seq, a_col, b_col)
    o_gdn, s_sample = _gdn_sample(gz, state_gdn_conv[e], conv_w, small, a_log_row, dt_row, gdn_g, state_gdn[e], o_gdn,
                                  rows_p, a_col, b_col)

    mixed = _matmul([o_nsa, o_gdn], [(w_out, e, QW, 0, 0), (w_out, e, GH, 1, 0)], (0, 1), _ep_sum, d, F32,
                    _col_tile(d, 512), "proj_out")

    kv_rows = proj[:, QW:QW + 4 * KVW]
    win_rows = proj[:, QW + 4 * KVW:QW + 6 * KVW]
    kv_p = proj
    kv_s = kv_rows[rows_p:rows_p + nd].reshape(nd, 1, 4, NSA_KV_HEADS, hd)
    nwin_p = min(NSA_WINDOW, seq)
    win_p = jnp.stack([win_rows[(b + 1) * seq - nwin_p:(b + 1) * seq] for b in range(batch)])
    win_p = win_p.reshape(batch, nwin_p, 2, NSA_KV_HEADS, hd)
    win_all = jnp.concatenate([cache_nsa_win[e], win_rows[rows_p:rows_p + nd].reshape(nd, 1, 2, NSA_KV_HEADS, hd)], axis=1)
    win_s = win_all[:, win_all.shape[1] - min(NSA_WINDOW, win_all.shape[1]):]
    gconv_p = jnp.stack([gz[(b + 1) * seq - (GDN_CONV - 1):(b + 1) * seq, 0:3 * GH] for b in range(batch)])
    gconv_s = jnp.concatenate([state_gdn_conv[e], gz[rows_p:rows_p + nd, 0:3 * GH][:, None]], axis=1)[:, 1:]
    return mixed, (kv_p, kv_s, win_p, win_s, s_prompt, s_sample, gconv_p, gconv_s)


def _odd_layer(hn, o, dims, state_conf_conv, w_pw1, w_dw, b_dw, ln_g, ln_b, w_pw2):
    batch, seq, nd, d = dims
    rows_p = batch * seq
    ch = w_dw.shape[1]
    tn = _col_tile(ch, 256)
    u = _matmul([hn], [(w_pw1, o, d, 0, 0), (w_pw1, o, d, 0, ch // tn)], (0, 0), _ep_glu, ch, F32, tn, "conf_pw1_glu")
    c = _dwconv_prompt(u, w_dw, b_dw, 0, ch, batch, seq, False, "conf_conv_prompt")
    c = _dwconv_sample(u, state_conf_conv[o], w_dw, b_dw, 0, ch, c, rows_p, False, "conf_conv_sample")
    y = _ln_silu(c, ln_g, ln_b)
    mixed = _matmul([y], [(w_pw2, o, ch, 0, 0)], (0,), _ep_plain, d, F32, _col_tile(d, 512), "conf_pw2")
    cc_p = jnp.stack([u[(b + 1) * seq - (CONF_WIDTH - 1):(b + 1) * seq] for b in range(batch)])
    cc_s = jnp.concatenate([state_conf_conv[o], u[rows_p:rows_p + nd][:, None]], axis=1)[:, 1:]
    return mixed, (cc_p, cc_s)


def kernel(x_prompt, x_sample, cache_nsa_kv, cache_nsa_win, state_gdn, state_gdn_conv, state_conf_conv, page_table,
           p_prompt, p_sample, norm_g, w_in_even, cmp_pool_w, gdn_conv_w, gdn_a_log, gdn_dt_bias, gdn_norm_g,
           w_out_even, w_pw1, w_dw, b_dw, conf_ln_g, conf_ln_b, w_pw2, w_ffn_gate, w_ffn_up, w_ffn_down,
           w_ple_gate, w_ple_proj):
    batch, seq, d = x_prompt.shape
    nd = x_sample.shape[0]
    depth = norm_g.shape[0]
    ffn = w_ffn_gate.shape[2]
    ple = w_ple_proj.shape[1]
    rows_p = batch * seq
    assert x_sample.shape[1] == 1 and nd <= 8 and rows_p % SAMPLE_PAD == 0
    dims = (batch, seq, nd, d)
    slopes = jnp.asarray(np.asarray([2.0 ** (-8.0 * (h + 1) / NSA_HEADS) for h in range(NSA_HEADS)], np.float32)
                         .reshape(NSA_KV_HEADS, NSA_GROUP))

    padrows = lambda a: jnp.concatenate([a, jnp.zeros((SAMPLE_PAD - nd,) + a.shape[1:], a.dtype)], axis=0)
    h = jnp.concatenate([x_prompt.reshape(rows_p, d), padrows(x_sample.reshape(nd, d))], axis=0)
    ones = jnp.ones((d,), F32)
    tn_f = _col_tile(ffn, 256)
    w_down_bf16 = w_ffn_down.astype(BF16)
    w_in_t = jnp.swapaxes(w_in_even, 1, 2)
    outs = [[] for _ in range(10)]
    hn = _norm(h, norm_g[0, 0])
    for i in range(depth):
        g_post, f_pre, f_post = norm_g[i, 1], norm_g[i, 2], norm_g[i, 3]
        if i % 2 == 0:
            e = i // 2
            mixed, st = _even_layer(hn, e, dims, slopes, cache_nsa_kv, cache_nsa_win, state_gdn, state_gdn_conv,
                                    page_table, w_in_t, cmp_pool_w[e], gdn_conv_w[e], gdn_a_log[e],
                                    gdn_dt_bias[e], gdn_norm_g[e], w_out_even)
            for lst, val in zip(outs[:8], st):
                lst.append(val)
        else:
            o = i // 2
            mixed, st = _odd_layer(hn, o, dims, state_conf_conv, w_pw1, w_dw[o], b_dw[o], conf_ln_g[o],
                                   conf_ln_b[o], w_pw2)
            outs[8].append(st[0])
            outs[9].append(st[1])
        h, hn = _addnorm(h, mixed, g_post, f_pre)
        act = _matmul([hn], [(w_ffn_gate, i, d, 0, 0), (w_ffn_up, i, d, 0, 0)], (0, 0), _ep_swiglu, ffn, BF16, tn_f,
                      "ffn_gate_up")
        f = _matmul([act], [(w_down_bf16, i, ffn, 0, 0)], (0,), _ep_plain, d, F32, _col_tile(d, 512), "ffn_down",
                    x_stationary=True)
        h, hn = _addnorm(h, f, f_post, ones)
        p_i = jnp.concatenate([p_prompt[i].reshape(rows_p, ple), padrows(p_sample[i].reshape(nd, ple))], axis=0).astype(BF16)
        h = _matmul([hn, p_i], [(w_ple_gate, i, d, 0, 0), (w_ple_proj, i, ple, 0, 0)], (0, 1), _ep_ple, d, F32,
                    _col_tile(d, 512), "ple", res=h)
        if i + 1 < depth:
            hn = _norm(h, norm_g[i + 1, 0])
    y_p = h[:rows_p].reshape(batch, seq, d)
    y_s = h[rows_p:rows_p + nd].reshape(nd, 1, d)
    kv_p = _kv_state(outs[0], QW, 4 * KVW, rows_p).reshape(len(outs[0]), batch, seq, 4, NSA_KV_HEADS, HEAD_DIM)
    return (y_p, y_s, kv_p) + tuple(jnp.stack(lst) for lst in outs[1:])
```
